```python
import jax, jax.numpy as jnp
from jax import lax
import numpy as np

D_MODEL = 2048
BATCH = 4
SEQ = 2048
DEPTH = 4
DEC_BATCH = 128
DEC_SEQ = 8
PAST_LEN = 8192
PAGE_SIZE = 128

N_A_LAYERS = DEPTH // 2
N_B_LAYERS = DEPTH - N_A_LAYERS
MLA_HEADS = 16
QK_NOPE = 128
QK_ROPE = 64
V_DIM = 128
Q_LORA = 768
KV_LORA = 512
MLA_ROW = KV_LORA + QK_ROPE
MLA_THETA = 10000.0
MLA_SCALE = (QK_NOPE + QK_ROPE) ** -0.5
SWA_HEADS = 32
SWA_KV_HEADS = 8
SWA_GROUP = SWA_HEADS // SWA_KV_HEADS
HEAD_DIM = 64
KV_WIDTH = SWA_KV_HEADS * HEAD_DIM
ROT_DIM = HEAD_DIM // 4
ROPE_THETA = 500000.0
WINDOW = 128
SWA_SCALE = HEAD_DIM ** -0.5
D_FF = 5632
CONV_W = 3
Q_BLOCK = 128
RMS_EPS = 1e-6
NEG_INF = -1e30

kernel_name = "yoco_mla_swa_sink_convffn_step"


def rms_norm(x, g):
    xf = x.astype(jnp.float32)
    y = xf * lax.rsqrt(jnp.mean(xf * xf, axis=-1, keepdims=True) + RMS_EPS)
    return (y * g.astype(jnp.float32)).astype(x.dtype)


def rope_cos_sin(pos, dim, theta):
    inv = theta ** (-jnp.arange(0, dim, 2, dtype=jnp.float32) / dim)
    ang = pos.astype(jnp.float32)[:, None] * inv[None, :]
    return jnp.cos(ang), jnp.sin(ang)


def apply_rope(x, cos, sin):
    extra = x.ndim - 3
    cos = cos.reshape(cos.shape[:1] + (1,) * extra + cos.shape[1:])
    sin = sin.reshape(sin.shape[:1] + (1,) * extra + sin.shape[1:])
    half = x.shape[-1] // 2
    x1 = x[..., :half].astype(jnp.float32)
    x2 = x[..., half:].astype(jnp.float32)
    out = jnp.concatenate([x1 * cos - x2 * sin, x2 * cos + x1 * sin], axis=-1)
    return out.astype(x.dtype)


def partial_rope(x, pos):
    cos, sin = rope_cos_sin(pos, ROT_DIM, ROPE_THETA)
    return jnp.concatenate([apply_rope(x[..., :ROT_DIM], cos, sin), x[..., ROT_DIM:]], axis=-1)


def mla_project(h, pos, w_in, q_g, kv_g, w_uq, w_uk):
    B, T, _ = h.shape
    z = h @ w_in
    c_q = rms_norm(z[..., :Q_LORA], q_g)
    c_kv = rms_norm(z[..., Q_LORA:Q_LORA + KV_LORA], kv_g)
    cos, sin = rope_cos_sin(pos, QK_ROPE, MLA_THETA)
    k_pe = apply_rope(z[..., Q_LORA + KV_LORA:], cos, sin)
    q = (c_q @ w_uq).reshape(B, T, MLA_HEADS, QK_NOPE + QK_ROPE)
    q_pe = apply_rope(q[..., QK_NOPE:], cos, sin)
    q_lat = jnp.einsum("bthn,chn->bthc", q[..., :QK_NOPE], w_uk)
    rows = jnp.concatenate([c_kv, k_pe], axis=-1)
    return q_lat, q_pe, rows


def mla_core(q_lat, q_pe, rows, mask):
    ckv = rows[..., :KV_LORA]
    kpe = rows[..., KV_LORA:]
    s = jnp.einsum("...qhc,...kc->...hqk", q_lat, ckv) + jnp.einsum("...qhr,...kr->...hqk", q_pe, kpe)
    s = jnp.where(mask, s.astype(jnp.float32) * MLA_SCALE, NEG_INF)
    p = jax.nn.softmax(s, axis=-1).astype(ckv.dtype)
    return jnp.einsum("...hqk,...kc->...qhc", p, ckv)


def mla_prompt_attend(q_lat, q_pe, rows):
    B, T = q_lat.shape[:2]
    nqb = T // Q_BLOCK
    ql = jnp.moveaxis(q_lat.reshape(B, nqb, Q_BLOCK, MLA_HEADS, KV_LORA), 1, 0)
    qp = jnp.moveaxis(q_pe.reshape(B, nqb, Q_BLOCK, MLA_HEADS, QK_ROPE), 1, 0)
    kpos = jnp.arange(T)

    def block(args):
        qlb, qpb, i = args
        qpos = i * Q_BLOCK + jnp.arange(Q_BLOCK)
        return mla_core(qlb, qpb, rows, kpos[None, :] <= qpos[:, None])

    out = lax.map(block, (ql, qp, jnp.arange(nqb)))
    return jnp.moveaxis(out, 0, 1).reshape(B, T, MLA_HEADS, KV_LORA)


def mla_sample_attend(q_lat, q_pe, rows, pool, page_table):
    T = q_lat.shape[1]
    n_past = page_table.shape[1] * PAGE_SIZE
    t = jnp.arange(T)
    mask = jnp.concatenate([jnp.ones((T, n_past), bool), t[None, :] <= t[:, None]], axis=1)

    def one_seq(args):
        ql, qp, new_rows, pages = args
        past = pool[pages].reshape(n_past, MLA_ROW).astype(new_rows.dtype)
        keys = jnp.concatenate([past, new_rows], axis=0)
        return mla_core(ql, qp, keys, mask)

    return lax.map(one_seq, (q_lat, q_pe, rows, page_table))


def mla_output(o_lat, w_uv, w_o):
    B, T = o_lat.shape[:2]
    v = jnp.einsum("bthc,chv->bthv", o_lat, w_uv)
    return v.reshape(B, T, MLA_HEADS * V_DIM) @ w_o


def swa_shared_kv(h, pos, g, w_kv, b_kv):
    B, T, _ = h.shape
    z = rms_norm(h, g) @ w_kv + b_kv
    k = z[..., :KV_WIDTH].reshape(B, T, SWA_KV_HEADS, HEAD_DIM)
    v = z[..., KV_WIDTH:].reshape(B, T, SWA_KV_HEADS, HEAD_DIM)
    return partial_rope(k, pos), v


def swa_query(h, pos, w_q, b_q):
    B, T, _ = h.shape
    q = (h @ w_q + b_q).reshape(B, T, SWA_KV_HEADS, SWA_GROUP, HEAD_DIM)
    return partial_rope(q, pos)


def sink_attention(q, k, v, mask, sinks):
    s = jnp.einsum("...qkgd,...skd->...kgqs", q, k).astype(jnp.float32) * SWA_SCALE
    s = jnp.where(mask[..., None, None, :, :], s, NEG_INF)
    sink = sinks.astype(jnp.float32)[:, :, None, None]
    m = jnp.maximum(s.max(axis=-1, keepdims=True), sink)
    e = jnp.exp(s - m)
    p = e / (e.sum(axis=-1, keepdims=True) + jnp.exp(sink - m))
    return jnp.einsum("...kgqs,...skd->...qkgd", p.astype(v.dtype), v)


def swa_prompt(q, k, v, sinks):
    B, T = q.shape[:2]
    nb = T // WINDOW
    qb = q.reshape(B, nb, WINDOW, SWA_KV_HEADS, SWA_GROUP, HEAD_DIM)

    def band(t):
        tb = t.reshape(B, nb, WINDOW, SWA_KV_HEADS, HEAD_DIM)
        prev = jnp.pad(tb, ((0, 0), (1, 0), (0, 0), (0, 0), (0, 0)))[:, :-1]
        return jnp.concatenate([prev, tb], axis=2)

    qi = jnp.arange(WINDOW) + WINDOW
    ki = jnp.arange(2 * WINDOW)
    d = qi[:, None] - ki[None, :]
    local = (d >= 0) & (d < WINDOW)
    has_prev = (jnp.arange(nb) > 0)[:, None, None] | (ki >= WINDOW)[None, None, :]
    mask = local[None] & has_prev
    out = sink_attention(qb, band(k), band(v), mask, sinks)
    return out.reshape(B, T, SWA_HEADS * HEAD_DIM)


def swa_sample(q, keys, vals, sinks):
    B, T = q.shape[:2]
    qpos = WINDOW + jnp.arange(T)
    kpos = jnp.arange(WINDOW + T)
    d = qpos[:, None] - kpos[None, :]
    mask = (d >= 0) & (d < WINDOW)
    out = sink_attention(q, keys, vals, mask, sinks)
    return out.reshape(B, T, SWA_HEADS * HEAD_DIM)


def conv_ffn(h, prev, w_up, conv_w, conv_b, w_down):
    T = h.shape[1]
    u = h @ w_up
    gate, up = u[..., :D_FF], u[..., D_FF:]
    ext = jnp.concatenate([prev.astype(gate.dtype), gate], axis=1)
    conv = conv_b + sum(ext[:, j:j + T] * conv_w[j] for j in range(CONV_W))
    y = (jax.nn.silu(conv) * up) @ w_down
    return y, ext[:, T:]


def setup_inputs(seed: int = 0) -> dict:
    key = jax.random.key(seed)
    ks = iter(jax.random.split(key, 40))

    def nrm(shape, scale=1.0):
        return jax.random.normal(next(ks), shape, jnp.float32) * scale

    def gain(shape):
        return 1.0 + nrm(shape, 0.05)

    n_pages = PAST_LEN // PAGE_SIZE
    n_used = DEC_BATCH * n_pages
    n_phys = n_used + n_used // 4
    x_prompt = nrm((BATCH, SEQ, D_MODEL))
    x_sample = nrm((DEC_BATCH, DEC_SEQ, D_MODEL))
    cache_mla = nrm((N_A_LAYERS, n_phys, PAGE_SIZE, MLA_ROW))
    state_swa_kv = nrm((DEC_BATCH, WINDOW, 2, SWA_KV_HEADS, HEAD_DIM))
    state_ffn = nrm((DEPTH, DEC_BATCH, CONV_W - 1, D_FF))
    page_table = jax.random.permutation(next(ks), n_phys)[:n_used].reshape(DEC_BATCH, n_pages).astype(jnp.int32)
    return {
        "x_prompt": x_prompt,
        "x_sample": x_sample,
        "cache_mla": cache_mla,
        "state_swa_kv": state_swa_kv,
        "state_ffn": state_ffn,
        "page_table": page_table,
        "ln_mix_pre": gain((DEPTH, D_MODEL)),
        "ln_mix_post": gain((DEPTH, D_MODEL)),
        "ln_ffn_pre": gain((DEPTH, D_MODEL)),
        "ln_ffn_post": gain((DEPTH, D_MODEL)),
        "w_ffn_up": nrm((DEPTH, D_MODEL, 2 * D_FF), D_MODEL ** -0.5),
        "ffn_conv_w": nrm((DEPTH, CONV_W, D_FF), CONV_W ** -0.5),
        "ffn_conv_b": nrm((DEPTH, D_FF), 0.02),
        "w_ffn_down": nrm((DEPTH, D_FF, D_MODEL), D_FF ** -0.5),
        "w_mla_in": nrm((N_A_LAYERS, D_MODEL, Q_LORA + KV_LORA + QK_ROPE), D_MODEL ** -0.5),
        "mla_q_norm": gain((N_A_LAYERS, Q_LORA)),
        "mla_kv_norm": gain((N_A_LAYERS, KV_LORA)),
        "w_mla_uq": nrm((N_A_LAYERS, Q_LORA, MLA_HEADS * (QK_NOPE + QK_ROPE)), Q_LORA ** -0.5),
        "w_mla_uk": nrm((N_A_LAYERS, KV_LORA, MLA_HEADS, QK_NOPE), KV_LORA ** -0.5),
        "w_mla_uv": nrm((N_A_LAYERS, KV_LORA, MLA_HEADS, V_DIM), KV_LORA ** -0.5),
        "w_mla_o": nrm((N_A_LAYERS, MLA_HEADS * V_DIM, D_MODEL), (MLA_HEADS * V_DIM) ** -0.5),
        "swa_kv_norm": gain((D_MODEL,)),
        "w_swa_kv": nrm((D_MODEL, 2 * KV_WIDTH), D_MODEL ** -0.5),
        "b_swa_kv": nrm((2 * KV_WIDTH,), 0.02),
        "w_swa_q": nrm((N_B_LAYERS, D_MODEL, SWA_HEADS * HEAD_DIM), D_MODEL ** -0.5),
        "b_swa_q": nrm((N_B_LAYERS, SWA_HEADS * HEAD_DIM), 0.02),
        "swa_sinks": nrm((N_B_LAYERS, SWA_KV_HEADS, SWA_GROUP), 0.5),
        "w_swa_o": nrm((N_B_LAYERS, SWA_HEADS * HEAD_DIM, D_MODEL), (SWA_HEADS * HEAD_DIM) ** -0.5),
        "b_swa_o": nrm((N_B_LAYERS, D_MODEL), 0.02),
    }


def reference(x_prompt, x_sample, cache_mla, state_swa_kv, state_ffn, page_table,
              ln_mix_pre, ln_mix_post, ln_ffn_pre, ln_ffn_post,
              w_ffn_up, ffn_conv_w, ffn_conv_b, w_ffn_down,
              w_mla_in, mla_q_norm, mla_kv_norm, w_mla_uq, w_mla_uk, w_mla_uv, w_mla_o,
              swa_kv_norm, w_swa_kv, b_swa_kv, w_swa_q, b_swa_q, swa_sinks, w_swa_o, b_swa_o):
    T_s = x_sample.shape[1]
    pos_p = jnp.arange(x_prompt.shape[1])
    pos_s = PAST_LEN + jnp.arange(T_s)
    xp, xs = x_prompt, x_sample
    mla_rows_p, mla_rows_s, ffn_p, ffn_s = [], [], [], []
    for l in range(DEPTH):
        if l == N_A_LAYERS:
            k_p, v_p = swa_shared_kv(xp, pos_p, swa_kv_norm, w_swa_kv, b_swa_kv)
            k_new, v_new = swa_shared_kv(xs, pos_s, swa_kv_norm, w_swa_kv, b_swa_kv)
            keys_s = jnp.concatenate([state_swa_kv[:, :, 0].astype(k_new.dtype), k_new], axis=1)
            vals_s = jnp.concatenate([state_swa_kv[:, :, 1].astype(v_new.dtype), v_new], axis=1)
            swa_kv_prompt = jnp.stack([k_p, v_p], axis=2)[:, -WINDOW:]
            swa_kv_sample = jnp.stack([keys_s, vals_s], axis=2)[:, T_s:]
        hp = rms_norm(xp, ln_mix_pre[l])
        hs = rms_norm(xs, ln_mix_pre[l])
        if l < N_A_LAYERS:
            a = l
            ql_p, qpe_p, rows_p = mla_project(hp, pos_p, w_mla_in[a], mla_q_norm[a], mla_kv_norm[a], w_mla_uq[a], w_mla_uk[a])
            ql_s, qpe_s, rows_s = mla_project(hs, pos_s, w_mla_in[a], mla_q_norm[a], mla_kv_norm[a], w_mla_uq[a], w_mla_uk[a])
            mix_p = mla_output(mla_prompt_attend(ql_p, qpe_p, rows_p), w_mla_uv[a], w_mla_o[a])
            mix_s = mla_output(mla_sample_attend(ql_s, qpe_s, rows_s, cache_mla[a], page_table), w_mla_uv[a], w_mla_o[a])
            mla_rows_p.append(rows_p)
            mla_rows_s.append(rows_s)
        else:
            b = l - N_A_LAYERS
            q_p = swa_query(hp, pos_p, w_swa_q[b], b_swa_q[b])
            q_s = swa_query(hs, pos_s, w_swa_q[b], b_swa_q[b])
            mix_p = swa_prompt(q_p, k_p, v_p, swa_sinks[b]) @ w_swa_o[b] + b_swa_o[b]
            mix_s = swa_sample(q_s, keys_s, vals_s, swa_sinks[b]) @ w_swa_o[b] + b_swa_o[b]
        xp = xp + rms_norm(mix_p, ln_mix_post[l])
        xs = xs + rms_norm(mix_s, ln_mix_post[l])
        hp = rms_norm(xp, ln_ffn_pre[l])
        hs = rms_norm(xs, ln_ffn_pre[l])
        prev_p = jnp.zeros((xp.shape[0], CONV_W - 1, D_FF), xp.dtype)
        f_p, st_p = conv_ffn(hp, prev_p, w_ffn_up[l], ffn_conv_w[l], ffn_conv_b[l], w_ffn_down[l])
        f_s, st_s = conv_ffn(hs, state_ffn[l], w_ffn_up[l], ffn_conv_w[l], ffn_conv_b[l], w_ffn_down[l])
        xp = xp + rms_norm(f_p, ln_ffn_post[l])
        xs = xs + rms_norm(f_s, ln_ffn_post[l])
        ffn_p.append(st_p)
        ffn_s.append(st_s)
    mla_rows_prompt = jnp.stack(mla_rows_p, axis=0)
    mla_rows_sample = jnp.stack(mla_rows_s, axis=0)
    ffn_state_prompt = jnp.stack(ffn_p, axis=0)
    ffn_state_sample = jnp.stack(ffn_s, axis=0)
    return (xp, xs, mla_rows_prompt, mla_rows_sample, swa_kv_prompt, swa_kv_sample, ffn_state_prompt, ffn_state_sample)
```

```python
import functools

import jax
import jax.numpy as jnp
from jax import lax
from jax.experimental import pallas as pl
from jax.experimental.pallas import tpu as pltpu

RMS_EPS = 1e-6
NEG_INF = -1e30
MLA_THETA = 10000.0
ROPE_THETA = 500000.0
LANES = 128
SUBLANES = 8
VMEM_LIMIT_BYTES = 56 * 1024 * 1024
BF16 = jnp.bfloat16
F32 = jnp.float32


def _params(n_axes):
    return pltpu.CompilerParams(dimension_semantics=("arbitrary",) * n_axes, vmem_limit_bytes=VMEM_LIMIT_BYTES)


def _const_spec(shape):
    nd = len(shape)
    return pl.BlockSpec(shape, lambda *_: (0,) * nd, pipeline_mode=pl.Buffered(1))


def _tile(m, pref):
    t = min(m, pref)
    while m % t or t % SUBLANES:
        t -= 1
    return t


def _dot(a, b):
    return jnp.dot(a, b, preferred_element_type=F32)


def _dot_nt(a, b):
    return lax.dot_general(a, b, (((1,), (1,)), ((), ())), preferred_element_type=F32)


def _rms(x, g):
    return x * lax.rsqrt(jnp.mean(x * x, axis=-1, keepdims=True) + RMS_EPS) * g


def _mla_in_kernel(x_ref, g_ref, w_ref, qg_ref, kvg_ref, tab_ref, rows_ref, cq_ref, ckv_ref, kpe_ref, *, q_lora, kv_lora, rope):
    h = _rms(x_ref[...], g_ref[...]).astype(BF16)
    z = _dot(h, w_ref[...])
    cq = _rms(z[:, :q_lora], qg_ref[...])
    ckv = _rms(z[:, q_lora:q_lora + kv_lora], kvg_ref[...])
    t = z[:, q_lora + kv_lora:] * tab_ref[...]
    kpe = (t + pltpu.roll(t, rope, axis=1))[:, :rope]
    cq_ref[...] = cq.astype(BF16)
    rows_ref[:, :kv_lora] = ckv
    rows_ref[:, kv_lora:] = kpe
    ckv_ref[...] = ckv.astype(BF16)
    kpe_ref[...] = kpe.astype(BF16)


def _mla_in(x, g, w_in_ext, q_g, kv_g, tab, *, q_lora, kv_lora, rope):
    m, d = x.shape
    tm = _tile(m, 512)
    n = w_in_ext.shape[1]
    row = lambda i: (i, 0)
    return pl.pallas_call(
        functools.partial(_mla_in_kernel, q_lora=q_lora, kv_lora=kv_lora, rope=rope),
        grid=(m // tm,),
        in_specs=[
            pl.BlockSpec((tm, d), row),
            _const_spec((1, d)),
            _const_spec((d, n)),
            _const_spec((1, q_lora)),
            _const_spec((1, kv_lora)),
            pl.BlockSpec((tm, 2 * rope), row),
        ],
        out_specs=[
            pl.BlockSpec((tm, kv_lora + rope), row),
            pl.BlockSpec((tm, q_lora), row),
            pl.BlockSpec((tm, kv_lora), row),
            pl.BlockSpec((tm, rope), row),
        ],
        out_shape=[
            jax.ShapeDtypeStruct((m, kv_lora + rope), F32),
            jax.ShapeDtypeStruct((m, q_lora), BF16),
            jax.ShapeDtypeStruct((m, kv_lora), BF16),
            jax.ShapeDtypeStruct((m, rope), BF16),
        ],
        compiler_params=_params(1),
        name="mla_in",
    )(x, g, w_in_ext, q_g, kv_g, tab)


def _mla_q_kernel(cq_ref, wuq_ref, wuk_ref, tab_ref, qlat_ref, qpe_ref, *, heads, nope, rope, scale):
    q = _dot(cq_ref[...], wuq_ref[...])
    tab = tab_ref[...]
    for h in range(heads):
        qn = q[:, h * nope:(h + 1) * nope].astype(BF16)
        qlat_ref[h] = (_dot(qn, wuk_ref[h]) * scale).astype(qlat_ref.dtype)
        lo = heads * nope + h * 2 * rope
        t = q[:, lo:lo + 2 * rope] * tab
        qpe_ref[h] = ((t + pltpu.roll(t, rope, axis=1))[:, :rope] * scale).astype(qpe_ref.dtype)


def _mla_q(cq, w_uq_ext, w_uk_t, tab, *, heads, nope, rope, scale, out_dtype):
    m, q_lora = cq.shape
    kv_lora = w_uk_t.shape[2]
    tm = _tile(m, 256)
    return pl.pallas_call(
        functools.partial(_mla_q_kernel, heads=heads, nope=nope, rope=rope, scale=scale),
        grid=(m // tm,),
        in_specs=[
            pl.BlockSpec((tm, q_lora), lambda i: (i, 0)),
            _const_spec(w_uq_ext.shape),
            _const_spec(w_uk_t.shape),
            pl.BlockSpec((tm, 2 * rope), lambda i: (i, 0)),
        ],
        out_specs=[
            pl.BlockSpec((heads, tm, kv_lora), lambda i: (0, i, 0)),
            pl.BlockSpec((heads, tm, rope), lambda i: (0, i, 0)),
        ],
        out_shape=[
            jax.ShapeDtypeStruct((heads, m, kv_lora), out_dtype),
            jax.ShapeDtypeStruct((heads, m, rope), out_dtype),
        ],
        compiler_params=_params(1),
        name="mla_q",
    )(cq, w_uq_ext, w_uk_t, tab)


def _mla_prompt_attn_kernel(qlat_ref, qpe_ref, ckv_ref, kpe_ref, o_ref, m_sc, l_sc, acc_sc, *, tq, tk, heads):
    i = pl.program_id(1)
    j = pl.program_id(2)
    nk = pl.num_programs(2)
    kv_lora = ckv_ref.shape[1]
    rope = kpe_ref.shape[1]
    last_tile = ((i + 1) * tq - 1) // tk

    @pl.when(j == 0)
    def _():
        m_sc[...] = jnp.full(m_sc.shape, NEG_INF, F32)
        l_sc[...] = jnp.zeros(l_sc.shape, F32)
        acc_sc[...] = jnp.zeros(acc_sc.shape, F32)

    @pl.when(j <= last_tile)
    def _():
        q = qlat_ref[...].reshape(heads * tq, kv_lora)
        qp = qpe_ref[...].reshape(heads * tq, rope)
        ckv = ckv_ref[...]
        s = _dot_nt(q, ckv) + _dot_nt(qp, kpe_ref[...])
        qpos = i * tq + lax.broadcasted_iota(jnp.int32, (tq, tk), 0)
        kpos = j * tk + lax.broadcasted_iota(jnp.int32, (tq, tk), 1)
        s = jnp.where((kpos <= qpos)[None], s.reshape(heads, tq, tk), NEG_INF).reshape(heads * tq, tk)
        m_prev = m_sc[...]
        m_new = jnp.maximum(m_prev, s.max(axis=1, keepdims=True))
        alpha = jnp.exp(m_prev - m_new)
        p = jnp.exp(s - m_new)
        l_sc[...] = alpha * l_sc[...] + p.sum(axis=1, keepdims=True)
        acc_sc[...] = alpha * acc_sc[...] + _dot(p.astype(BF16), ckv)
        m_sc[...] = m_new

    @pl.when(j == nk - 1)
    def _():
        o = acc_sc[...] / l_sc[...]
        o_ref[...] = o.reshape(heads, tq, kv_lora).astype(o_ref.dtype)


def _mla_prompt_attn(qlat, qpe, ckv, kpe, *, batch, seq):
    heads, m, kv_lora = qlat.shape
    rope = qpe.shape[2]
    tq = _tile(seq, 128)
    tk = _tile(seq, 512)
    nq, nk = seq // tq, seq // tk

    def kmap(b, i, j):
        return (b * nk + jnp.minimum(j, ((i + 1) * tq - 1) // tk), 0)

    qmap = lambda b, i, j: (0, b * nq + i, 0)
    return pl.pallas_call(
        functools.partial(_mla_prompt_attn_kernel, tq=tq, tk=tk, heads=heads),
        grid=(batch, nq, nk),
        in_specs=[
            pl.BlockSpec((heads, tq, kv_lora), qmap),
            pl.BlockSpec((heads, tq, rope), qmap),
            pl.BlockSpec((tk, kv_lora), kmap),
            pl.BlockSpec((tk, rope), kmap),
        ],
        out_specs=pl.BlockSpec((heads, tq, kv_lora), qmap),
        out_shape=jax.ShapeDtypeStruct((heads, m, kv_lora), BF16),
        scratch_shapes=[
            pltpu.VMEM((heads * tq, 1), F32),
            pltpu.VMEM((heads * tq, 1), F32),
            pltpu.VMEM((heads * tq, kv_lora), F32),
        ],
        compiler_params=_params(3),
        name="mla_prompt_attn",
    )(qlat, qpe, ckv, kpe)


def _mla_sample_attn_kernel(pt_ref, qlat_ref, qpe_ref, new_ref, *rest, pg, heads, tq, page):
    del pt_ref
    page_refs = rest[:pg]
    o_ref, m_sc, l_sc, acc_sc = rest[pg:]
    j = pl.program_id(1)
    nj = pl.num_programs(1)
    kv_lora = qlat_ref.shape[2]
    rope = qpe_ref.shape[2]
    rows = heads * tq

    @pl.when(j == 0)
    def _():
        m_sc[...] = jnp.full(m_sc.shape, NEG_INF, F32)
        l_sc[...] = jnp.zeros(l_sc.shape, F32)
        acc_sc[...] = jnp.zeros(acc_sc.shape, F32)

    q = qlat_ref[...].reshape(rows, kv_lora).astype(BF16)
    qp = qpe_ref[...].reshape(rows, rope).astype(BF16)

    def scores(k):
        ckv = k[:, :kv_lora].astype(BF16)
        return _dot_nt(q, ckv) + _dot_nt(qp, k[:, kv_lora:].astype(BF16)), ckv

    def update(s_list, v_list):
        m_prev = m_sc[...]
        m_new = m_prev
        for s in s_list:
            m_new = jnp.maximum(m_new, s.max(axis=1, keepdims=True))
        alpha = jnp.exp(m_prev - m_new)
        l = alpha * l_sc[...]
        acc = alpha * acc_sc[...]
        for s, v in zip(s_list, v_list):
            p = jnp.exp(s - m_new)
            l = l + p.sum(axis=1, keepdims=True)
            acc = acc + _dot(p.astype(BF16), v)
        m_sc[...] = m_new
        l_sc[...] = l
        acc_sc[...] = acc

    s_list, v_list = [], []
    for r in page_refs:
        s, ckv = scores(r[...])
        s_list.append(s)
        v_list.append(ckv)
    update(s_list, v_list)

    @pl.when(j == nj - 1)
    def _():
        k_new = jnp.concatenate([new_ref[...], jnp.zeros((page - tq, kv_lora + rope), F32)], axis=0)
        s, ckv = scores(k_new)
        t_q = lax.broadcasted_iota(jnp.int32, (tq, page), 0)
        t_k = lax.broadcasted_iota(jnp.int32, (tq, page), 1)
        s = jnp.where((t_k <= t_q)[None], s.reshape(heads, tq, page), NEG_INF).reshape(rows, page)
        update([s], [ckv])
        o = acc_sc[...] / l_sc[...]
        o_ref[...] = o.reshape(heads, tq, kv_lora).astype(o_ref.dtype)


def _mla_sample_attn(qlat, qpe, rows_new, cache, page_table, *, layer, tq, pages_per_step):
    heads, m, kv_lora = qlat.shape
    rope = qpe.shape[2]
    n_seq, n_pages = page_table.shape
    page = cache.shape[2]
    pg = pages_per_step
    while n_pages % pg:
        pg -= 1
    qmap = lambda s, j, pt: (0, s, 0)

    def page_spec(k):
        return pl.BlockSpec((None, None, page, kv_lora + rope), lambda s, j, pt: (layer, pt[s * n_pages + j * pg + k], 0, 0))

    grid_spec = pltpu.PrefetchScalarGridSpec(
        num_scalar_prefetch=1,
        grid=(n_seq, n_pages // pg),
        in_specs=[
            pl.BlockSpec((heads, tq, kv_lora), qmap),
            pl.BlockSpec((heads, tq, rope), qmap),
            pl.BlockSpec((tq, kv_lora + rope), lambda s, j, pt: (s, 0)),
        ] + [page_spec(k) for k in range(pg)],
        out_specs=pl.BlockSpec((heads, tq, kv_lora), qmap),
        scratch_shapes=[
            pltpu.VMEM((heads * tq, 1), F32),
            pltpu.VMEM((heads * tq, 1), F32),
            pltpu.VMEM((heads * tq, kv_lora), F32),
        ],
    )
    return pl.pallas_call(
        functools.partial(_mla_sample_attn_kernel, pg=pg, heads=heads, tq=tq, page=page),
        grid_spec=grid_spec,
        out_shape=jax.ShapeDtypeStruct((heads, m, kv_lora), F32),
        compiler_params=_params(2),
        name="mla_sample_attn",
    )(page_table.reshape(-1), qlat, qpe, rows_new, *([cache] * pg))


def _mla_out_kernel(o_ref, wuv_ref, wo_ref, x_ref, g_ref, out_ref, *, heads):
    v = jnp.concatenate([_dot(o_ref[h].astype(BF16), wuv_ref[h]).astype(BF16) for h in range(heads)], axis=1)
    out_ref[...] = x_ref[...] + _rms(_dot(v, wo_ref[...]), g_ref[...])


def _mla_out(o_lat, w_uv_t, w_o, x, g):
    heads, m, kv_lora = o_lat.shape
    d = x.shape[1]
    tm = _tile(m, 256)
    return pl.pallas_call(
        functools.partial(_mla_out_kernel, heads=heads),
        grid=(m // tm,),
        in_specs=[
            pl.BlockSpec((heads, tm, kv_lora), lambda i: (0, i, 0)),
            _const_spec(w_uv_t.shape),
            _const_spec(w_o.shape),
            pl.BlockSpec((tm, d), lambda i: (i, 0)),
            _const_spec((1, d)),
        ],
        out_specs=pl.BlockSpec((tm, d), lambda i: (i, 0)),
        out_shape=jax.ShapeDtypeStruct((m, d), F32),
        compiler_params=_params(1),
        name="mla_out",
    )(o_lat, w_uv_t, w_o, x, g)


def _out_proj_kernel(a_ref, w_ref, b_ref, x_ref, g_ref, out_ref):
    y = _dot(a_ref[...].astype(BF16), w_ref[...]) + b_ref[...]
    out_ref[...] = x_ref[...] + _rms(y, g_ref[...])


def _out_proj(a, w, b, x, g):
    m, k = a.shape
    d = x.shape[1]
    tm = _tile(m, 512)
    return pl.pallas_call(
        _out_proj_kernel,
        grid=(m // tm,),
        in_specs=[
            pl.BlockSpec((tm, k), lambda i: (i, 0)),
            _const_spec(w.shape),
            _const_spec((1, d)),
            pl.BlockSpec((tm, d), lambda i: (i, 0)),
            _const_spec((1, d)),
        ],
        out_specs=pl.BlockSpec((tm, d), lambda i: (i, 0)),
        out_shape=jax.ShapeDtypeStruct((m, d), F32),
        compiler_params=_params(1),
        name="out_proj",
    )(a, w, b, x, g)


def _norm_matmul_kernel(*refs, has_bias, has_rope, rot_half, scale):
    x_ref, g_ref, w_ref = refs[:3]
    refs = refs[3:]
    if has_bias:
        b_ref, refs = refs[0], refs[1:]
    if has_rope:
        (c_ref, s1_ref, s2_ref), refs = refs[:3], refs[3:]
    out_ref, h_sc = refs

    @pl.when(pl.program_id(1) == 0)
    def _():
        h_sc[...] = _rms(x_ref[...], g_ref[...]).astype(BF16)

    y = _dot(h_sc[...], w_ref[...])
    if has_bias:
        y = y + b_ref[...]
    if has_rope:
        c, s1, s2 = c_ref[...], s1_ref[...], s2_ref[...]
        for k in range(y.shape[1] // LANES):
            yc = y[:, k * LANES:(k + 1) * LANES]
            yc = yc * c + pltpu.roll(yc, LANES - rot_half, axis=1) * s1 + pltpu.roll(yc, rot_half, axis=1) * s2
            out_ref[:, k * LANES:(k + 1) * LANES] = (yc * scale).astype(out_ref.dtype)
    else:
        out_ref[...] = (y * scale).astype(out_ref.dtype) if scale != 1.0 else y.astype(out_ref.dtype)


def _norm_matmul(x, g, w, b=None, rope_tabs=None, *, rot_half=0, scale=1.0, out_dtype=F32, tm_pref=1024, tn_pref=1024):
    m, d = x.shape
    n = w.shape[1]
    tm = _tile(m, tm_pref)
    tn = min(n, tn_pref)
    while n % tn or tn % LANES:
        tn -= LANES
    in_specs = [
        pl.BlockSpec((tm, d), lambda i, j: (i, 0)),
        _const_spec((1, d)),
        pl.BlockSpec((d, tn), lambda i, j: (0, j)),
    ]
    args = [x, g, w]
    if b is not None:
        in_specs.append(pl.BlockSpec((1, tn), lambda i, j: (0, j)))
        args.append(b)
    if rope_tabs is not None:
        in_specs += [pl.BlockSpec((tm, LANES), lambda i, j: (i, 0))] * 3
        args += list(rope_tabs)
    return pl.pallas_call(
        functools.partial(_norm_matmul_kernel, has_bias=b is not None, has_rope=rope_tabs is not None, rot_half=rot_half, scale=scale),
        grid=(m // tm, n // tn),
        in_specs=in_specs,
        out_specs=pl.BlockSpec((tm, tn), lambda i, j: (i, j)),
        out_shape=jax.ShapeDtypeStruct((m, n), out_dtype),
        scratch_shapes=[pltpu.VMEM((tm, d), BF16)],
        compiler_params=_params(2),
        name="norm_matmul",
    )(*args)


def _ffn_down_kernel(gate_ref, prev_ref, up_ref, cw_ref, cb_ref, wd_ref, x_ref, g_ref, out_ref, acc_sc, *, per_seq_rows, tiles_per_seq):
    i = pl.program_id(0)
    f = pl.program_id(1)
    nf = pl.num_programs(1)
    gate = gate_ref[...]
    tm = gate.shape[0]
    r = lax.broadcasted_iota(jnp.int32, gate.shape, 0)
    d1 = pltpu.roll(gate, 1, axis=0)
    d2 = pltpu.roll(gate, 2, axis=0)
    if tiles_per_seq:
        live = i % tiles_per_seq != 0
        h6 = jnp.where(live, prev_ref[SUBLANES - 2:SUBLANES - 1, :], 0.0)
        h7 = jnp.where(live, prev_ref[SUBLANES - 1:SUBLANES, :], 0.0)
        g1 = jnp.where(r == 0, h7, d1)
        g2 = jnp.where(r == 0, h6, jnp.where(r == 1, h7, d2))
    else:
        t = r % per_seq_rows
        p2 = prev_ref[...]
        g1 = jnp.where(t == 0, pltpu.roll(p2, tm - 1, axis=0), d1)
        g2 = jnp.where(t < 2, p2, d2)
    conv = cb_ref[...] + g2 * cw_ref[0:1, :] + g1 * cw_ref[1:2, :] + gate * cw_ref[2:3, :]
    act = conv * (1.0 / (1.0 + jnp.exp(-conv))) * up_ref[...]
    part = _dot(act.astype(BF16), wd_ref[...])

    @pl.when(f == 0)
    def _():
        acc_sc[...] = part

    @pl.when(f > 0)
    def _():
        acc_sc[...] += part

    @pl.when(f == nf - 1)
    def _():
        out_ref[...] = x_ref[...] + _rms(acc_sc[...], g_ref[...])


def _ffn_down(u, prev, conv_w, conv_b, w_down, x, g, *, seq_rows):
    m, d = x.shape
    d_ff = w_down.shape[0]
    tf = _tile(d_ff, 512)
    nf = d_ff // tf
    if prev is None:
        tm = _tile(seq_rows, 512)
        tiles_per_seq = seq_rows // tm
        hb = tm // SUBLANES
        prev_arr = u
        prev_spec = pl.BlockSpec((SUBLANES, tf), lambda i, f: (jnp.maximum(i * hb - 1, 0), f))
    else:
        tm = _tile(m, 512)
        assert tm % seq_rows == 0 and seq_rows >= 2
        tiles_per_seq = 0
        prev_arr = prev
        prev_spec = pl.BlockSpec((tm, tf), lambda i, f: (i, f))
    return pl.pallas_call(
        functools.partial(_ffn_down_kernel, per_seq_rows=seq_rows, tiles_per_seq=tiles_per_seq),
        grid=(m // tm, nf),
        in_specs=[
            pl.BlockSpec((tm, tf), lambda i, f: (i, f)),
            prev_spec,
            pl.BlockSpec((tm, tf), lambda i, f: (i, nf + f)),
            pl.BlockSpec((conv_w.shape[0], tf), lambda i, f: (0, f)),
            pl.BlockSpec((1, tf), lambda i, f: (0, f)),
            pl.BlockSpec((tf, d), lambda i, f: (f, 0)),
            pl.BlockSpec((tm, d), lambda i, f: (i, 0)),
            _const_spec((1, d)),
        ],
        out_specs=pl.BlockSpec((tm, d), lambda i, f: (i, 0)),
        out_shape=jax.ShapeDtypeStruct((m, d), F32),
        scratch_shapes=[pltpu.VMEM((tm, d), F32)],
        compiler_params=_params(2),
        name="ffn_down",
    )(u, prev_arr, u, conv_w, conv_b, w_down, x, g)


def _sink_softmax_pv(s_list, v_list, sink):
    m = sink
    for s in s_list:
        m = jnp.maximum(m, s.max(axis=1, keepdims=True))
    denom = jnp.exp(sink - m)
    es = []
    for s in s_list:
        e = jnp.exp(s - m)
        es.append(e)
        denom = denom + e.sum(axis=1, keepdims=True)
    o = None
    for e, v in zip(es, v_list):
        pv = _dot((e / denom).astype(BF16), v)
        o = pv if o is None else o + pv
    return o


def _swa_prompt_kernel(sink_ref, q_ref, kc_ref, kp_ref, vc_ref, vp_ref, o_ref, *, kv_heads, group, hd):
    i = pl.program_id(1)
    w = q_ref.shape[0]
    qi = lax.broadcasted_iota(jnp.int32, (w, 2 * w), 0) + w
    ki = lax.broadcasted_iota(jnp.int32, (w, 2 * w), 1)
    dist = qi - ki
    mask = (dist >= 0) & (dist < w) & ((ki >= w) | (i > 0))
    for kv in range(kv_heads):
        sl = slice(kv * hd, (kv + 1) * hd)
        kk = jnp.concatenate([kp_ref[:, sl], kc_ref[:, sl]], axis=0).astype(BF16)
        vv = jnp.concatenate([vp_ref[:, sl], vc_ref[:, sl]], axis=0).astype(BF16)
        outs = []
        for g in range(group):
            h = kv * group + g
            s = jnp.where(mask, _dot_nt(q_ref[:, h * hd:(h + 1) * hd], kk), NEG_INF)
            outs.append(_sink_softmax_pv([s], [vv], sink_ref[h]))
        o_ref[:, kv * group * hd:(kv + 1) * group * hd] = jnp.concatenate(outs, axis=1).astype(o_ref.dtype)


def _swa_prompt_attn(q, k, v, sinks, *, batch, seq, window, kv_heads, hd):
    m, dq = q.shape
    dk = k.shape[1]
    nb = seq // window
    group = dq // (kv_heads * hd)
    cur = lambda b, i, sk: (b * nb + i, 0)
    prev = lambda b, i, sk: (jnp.maximum(b * nb + i - 1, 0), 0)
    grid_spec = pltpu.PrefetchScalarGridSpec(
        num_scalar_prefetch=1,
        grid=(batch, nb),
        in_specs=[
            pl.BlockSpec((window, dq), cur),
            pl.BlockSpec((window, dk), cur),
            pl.BlockSpec((window, dk), prev),
            pl.BlockSpec((window, dk), cur),
            pl.BlockSpec((window, dk), prev),
        ],
        out_specs=pl.BlockSpec((window, dq), cur),
    )
    return pl.pallas_call(
        functools.partial(_swa_prompt_kernel, kv_heads=kv_heads, group=group, hd=hd),
        grid_spec=grid_spec,
        out_shape=jax.ShapeDtypeStruct((m, dq), BF16),
        compiler_params=_params(2),
        name="swa_prompt_attn",
    )(sinks.reshape(-1), q, k, k, v, v)


def _swa_sample_kernel(sink_ref, q_ref, st_ref, kn_ref, vn_ref, o_ref, *, kv_heads, group, hd):
    tq = q_ref.shape[0]
    w = st_ref.shape[0]
    dk = kv_heads * hd
    rows = group * tq
    t_q = lax.broadcasted_iota(jnp.int32, (rows, w), 0) % tq
    col = lax.broadcasted_iota(jnp.int32, (rows, w), 1)
    grp = lax.broadcasted_iota(jnp.int32, (rows, 1), 0) // tq
    pad = jnp.zeros((w - tq, hd), F32)
    outs = []
    for kv in range(kv_heads):
        sl = slice(kv * hd, (kv + 1) * hd)
        k_old = st_ref[:, sl].astype(BF16)
        v_old = st_ref[:, dk + kv * hd:dk + (kv + 1) * hd].astype(BF16)
        k_new = jnp.concatenate([kn_ref[:, sl], pad], axis=0).astype(BF16)
        v_new = jnp.concatenate([vn_ref[:, sl], pad], axis=0).astype(BF16)
        qs = jnp.concatenate([q_ref[:, (kv * group + g) * hd:(kv * group + g + 1) * hd] for g in range(group)], axis=0).astype(BF16)
        sink = jnp.zeros((rows, 1), F32)
        for g in range(group):
            sink = jnp.where(grp == g, sink_ref[kv * group + g], sink)
        s_old = jnp.where(col > t_q, _dot_nt(qs, k_old), NEG_INF)
        s_new = jnp.where(col <= t_q, _dot_nt(qs, k_new), NEG_INF)
        o = _sink_softmax_pv([s_old, s_new], [v_old, v_new], sink)
        outs += [o[g * tq:(g + 1) * tq] for g in range(group)]
    o_ref[...] = jnp.concatenate(outs, axis=1).astype(o_ref.dtype)


def _swa_sample_attn(q, state, k_new, v_new, sinks, *, tq, kv_heads, hd):
    m, dq = q.shape
    n_seq, window, dkv = state.shape
    dk = k_new.shape[1]
    group = dq // (kv_heads * hd)
    row = lambda s, sk: (s, 0)
    grid_spec = pltpu.PrefetchScalarGridSpec(
        num_scalar_prefetch=1,
        grid=(n_seq,),
        in_specs=[
            pl.BlockSpec((tq, dq), row),
            pl.BlockSpec((None, window, dkv), lambda s, sk: (s, 0, 0)),
            pl.BlockSpec((tq, dk), row),
            pl.BlockSpec((tq, dk), row),
        ],
        out_specs=pl.BlockSpec((tq, dq), row),
    )
    return pl.pallas_call(
        functools.partial(_swa_sample_kernel, kv_heads=kv_heads, group=group, hd=hd),
        grid_spec=grid_spec,
        out_shape=jax.ShapeDtypeStruct((m, dq), F32),
        compiler_params=_params(1),
        name="swa_sample_attn",
    )(sinks.reshape(-1), q, state, k_new, v_new)


def _rope_tables(pos, dim, theta):
    inv = theta ** (-jnp.arange(0, dim, 2, dtype=F32) / dim)
    ang = pos.astype(F32)[:, None] * inv[None, :]
    return jnp.cos(ang), jnp.sin(ang)


def _rot_cols(w, half):
    return jnp.concatenate([-w[..., half:], w[..., :half]], axis=-1)


def kernel(x_prompt, x_sample, cache_mla, state_swa_kv, state_ffn, page_table, ln_mix_pre, ln_mix_post, ln_ffn_pre, ln_ffn_post, w_ffn_up, ffn_conv_w, ffn_conv_b, w_ffn_down, w_mla_in, mla_q_norm, mla_kv_norm, w_mla_uq, w_mla_uk, w_mla_uv, w_mla_o, swa_kv_norm, w_swa_kv, b_swa_kv, w_swa_q, b_swa_q, swa_sinks, w_swa_o, b_swa_o):
    batch, seq, d = x_prompt.shape
    n_seq, tq_s, _ = x_sample.shape
    depth = ln_mix_pre.shape[0]
    n_a = w_mla_in.shape[0]
    kv_lora, heads, nope = w_mla_uk.shape[1:]
    v_dim = w_mla_uv.shape[3]
    q_lora = mla_q_norm.shape[1]
    rope = w_mla_in.shape[2] - q_lora - kv_lora
    page = cache_mla.shape[2]
    past_len = page_table.shape[1] * page
    window, _, kv_heads, hd = state_swa_kv.shape[1:]
    dk = kv_heads * hd
    rot_half = hd // 8
    d_ff = w_ffn_down.shape[1]
    mla_scale = float((nope + rope) ** -0.5)
    swa_scale = float(hd ** -0.5)
    assert 2 * rope == LANES and 2 * hd == LANES and tq_s == SUBLANES and seq % window == 0

    xs = {"p": x_prompt.reshape(batch * seq, d), "s": x_sample.reshape(n_seq * tq_s, d)}
    pos = {"p": jnp.tile(jnp.arange(seq), batch), "s": jnp.tile(past_len + jnp.arange(tq_s), n_seq)}
    mla_tab, swa_tabs = {}, {}
    for st in ("p", "s"):
        cos, sin = _rope_tables(pos[st], rope, MLA_THETA)
        mla_tab[st] = jnp.concatenate([cos, cos, sin, sin], axis=1)
        cos, sin = _rope_tables(pos[st], 2 * rot_half, ROPE_THETA)
        one = jnp.ones((cos.shape[0], hd - 2 * rot_half), F32)
        zero = jnp.zeros_like(one)
        z8 = jnp.zeros_like(sin)
        swa_tabs[st] = tuple(
            jnp.tile(jnp.concatenate(parts, axis=1), (1, LANES // hd))
            for parts in ([cos, cos, one], [-sin, z8, zero], [z8, sin, zero])
        )
    row2 = lambda a: a.reshape(1, -1)

    mla_rows = {"p": [], "s": []}
    ffn_state = {"p": [], "s": []}
    k_sh, v_sh = {}, {}
    swa_state = state_swa_kv.reshape(n_seq, window, 2 * dk)
    for l in range(depth):
        if l < n_a:
            a = l
            w_in = w_mla_in[a]
            w_pe = w_in[:, q_lora + kv_lora:]
            w_in_ext = jnp.concatenate([w_in, _rot_cols(w_pe, rope // 2)], axis=1).astype(BF16)
            w_uq = w_mla_uq[a].reshape(q_lora, heads, nope + rope)
            w_uq_pe = w_uq[..., nope:]
            w_uq_ext = jnp.concatenate(
                [w_uq[..., :nope].reshape(q_lora, heads * nope),
                 jnp.concatenate([w_uq_pe, _rot_cols(w_uq_pe, rope // 2)], axis=-1).reshape(q_lora, heads * 2 * rope)],
                axis=1).astype(BF16)
            w_uk_t = jnp.transpose(w_mla_uk[a], (1, 2, 0)).astype(BF16)
            w_uv_t = jnp.transpose(w_mla_uv[a], (1, 0, 2)).astype(BF16)
            w_o = w_mla_o[a].astype(BF16)
            for st in ("p", "s"):
                rows, cq, ckv, kpe = _mla_in(xs[st], row2(ln_mix_pre[l]), w_in_ext, row2(mla_q_norm[a]), row2(mla_kv_norm[a]),
                                             mla_tab[st], q_lora=q_lora, kv_lora=kv_lora, rope=rope)
                mla_rows[st].append(rows)
                qlat, qpe = _mla_q(cq, w_uq_ext, w_uk_t, mla_tab[st], heads=heads, nope=nope, rope=rope, scale=mla_scale,
                                   out_dtype=BF16 if st == "p" else F32)
                if st == "p":
                    o_lat = _mla_prompt_attn(qlat, qpe, ckv, kpe, batch=batch, seq=seq)
                else:
                    o_lat = _mla_sample_attn(qlat, qpe, rows, cache_mla, page_table, layer=a, tq=tq_s, pages_per_step=16)
                xs[st] = _mla_out(o_lat, w_uv_t, w_o, xs[st], row2(ln_mix_post[l]))
        else:
            b = l - n_a
            if l == n_a:
                w_k = w_swa_kv[:, :dk].astype(BF16)
                w_v = w_swa_kv[:, dk:].astype(BF16)
                for st in ("p", "s"):
                    k_sh[st] = _norm_matmul(xs[st], row2(swa_kv_norm), w_k, row2(b_swa_kv[:dk]), swa_tabs[st], rot_half=rot_half)
                    v_sh[st] = _norm_matmul(xs[st], row2(swa_kv_norm), w_v, row2(b_swa_kv[dk:]))
            w_q = w_swa_q[b].astype(BF16)
            w_o = w_swa_o[b].astype(BF16)
            for st in ("p", "s"):
                q = _norm_matmul(xs[st], row2(ln_mix_pre[l]), w_q, row2(b_swa_q[b]), swa_tabs[st], rot_half=rot_half,
                                 scale=swa_scale, out_dtype=BF16 if st == "p" else F32)
                if st == "p":
                    attn = _swa_prompt_attn(q, k_sh[st], v_sh[st], swa_sinks[b], batch=batch, seq=seq, window=window,
                                            kv_heads=kv_heads, hd=hd)
                else:
                    attn = _swa_sample_attn(q, swa_state, k_sh[st], v_sh[st], swa_sinks[b], tq=tq_s, kv_heads=kv_heads, hd=hd)
                xs[st] = _out_proj(attn, w_o, row2(b_swa_o[b]), xs[st], row2(ln_mix_post[l]))
        w_up = w_ffn_up[l].astype(BF16)
        w_down = w_ffn_down[l].astype(BF16)
        for st in ("p", "s"):
            u = _norm_matmul(xs[st], row2(ln_ffn_pre[l]), w_up)
            if st == "p":
                prev = None
                ffn_state[st].append(u.reshape(batch, seq, 2 * d_ff)[:, seq - 2:, :d_ff])
            else:
                prev = jnp.pad(state_ffn[l], ((0, 0), (0, tq_s - state_ffn.shape[2]), (0, 0))).reshape(n_seq * tq_s, d_ff)
                ffn_state[st].append(u.reshape(n_seq, tq_s, 2 * d_ff)[:, tq_s - 2:, :d_ff])
            xs[st] = _ffn_down(u, prev, ffn_conv_w[l], row2(ffn_conv_b[l]), w_down, xs[st], row2(ln_ffn_post[l]),
                               seq_rows=seq if st == "p" else tq_s)

    kv_p = jnp.stack([k_sh["p"].reshape(batch, seq, kv_heads, hd), v_sh["p"].reshape(batch, seq, kv_heads, hd)], axis=2)
    kv_s_new = jnp.stack([k_sh["s"].reshape(n_seq, tq_s, kv_heads, hd), v_sh["s"].reshape(n_seq, tq_s, kv_heads, hd)], axis=2)
    return (
        xs["p"].reshape(batch, seq, d),
        xs["s"].reshape(n_seq, tq_s, d),
        jnp.stack(mla_rows["p"], axis=0).reshape(n_a, batch, seq, kv_lora + rope),
        jnp.stack(mla_rows["s"], axis=0).reshape(n_a, n_seq, tq_s, kv_lora + rope),
        kv_p[:, seq - window:],
        jnp.concatenate([state_swa_kv, kv_s_new], axis=1)[:, tq_s:],
        jnp.stack(ffn_state["p"], axis=0),
        jnp.stack(ffn_state["s"], axis=0),
    )
```

```python
import functools

import jax
import jax.numpy as jnp
from jax import lax
from jax.experimental import pallas as pl
from jax.experimental.pallas import tpu as pltpu

RMS_EPS = 1e-6
NEG_INF = -1e30
MLA_THETA = 10000.0
ROPE_THETA = 500000.0
LANES = 128
SUBLANES = 8
VMEM_LIMIT_BYTES = 56 * 1024 * 1024
MLA_PAGES_PER_STEP = 32
LOG2E = 1.4426950408889634
BF16 = jnp.bfloat16
F32 = jnp.float32


def _params(n_axes):
    return pltpu.CompilerParams(dimension_semantics=("arbitrary",) * n_axes, vmem_limit_bytes=VMEM_LIMIT_BYTES)


def _const_spec(shape):
    nd = len(shape)
    return pl.BlockSpec(shape, lambda *_: (0,) * nd, pipeline_mode=pl.Buffered(1))


def _tile(m, pref):
    t = min(m, pref)
    while m % t or t % SUBLANES:
        t -= 1
    return t


def _dot(a, b):
    return jnp.dot(a, b, preferred_element_type=F32)


def _dot_nt(a, b):
    return lax.dot_general(a, b, (((1,), (1,)), ((), ())), preferred_element_type=F32)


def _rms(x, g):
    return x * lax.rsqrt(jnp.mean(x * x, axis=-1, keepdims=True) + RMS_EPS) * g


def _mla_in_kernel(x_ref, g_ref, w_ref, qg_ref, kvg_ref, tab_ref, rows_ref, cq_ref, *key_refs, q_lora, kv_lora, rope):
    h = _rms(x_ref[...], g_ref[...]).astype(BF16)
    z = _dot(h, w_ref[...])
    cq = _rms(z[:, :q_lora], qg_ref[...])
    ckv = _rms(z[:, q_lora:q_lora + kv_lora], kvg_ref[...])
    t = z[:, q_lora + kv_lora:] * tab_ref[...]
    kpe2 = t + pltpu.roll(t, rope, axis=1)
    cq_ref[...] = cq.astype(BF16)
    rows_ref[:, :kv_lora] = ckv
    rows_ref[:, kv_lora:] = kpe2[:, :rope]
    if key_refs:
        ckv_ref, kt_ref = key_refs
        ckv_ref[...] = ckv.astype(BF16)
        kt_ref[:kv_lora, :] = ckv.T.astype(BF16)
        kt_ref[kv_lora:, :] = kpe2.T[:rope].astype(BF16)


def _mla_in(x, g, w_in_ext, q_g, kv_g, tab, *, q_lora, kv_lora, rope, with_keys):
    m, d = x.shape
    tm = _tile(m, 512)
    n = w_in_ext.shape[1]
    row = lambda i: (i, 0)
    out_specs = [pl.BlockSpec((tm, kv_lora + rope), row), pl.BlockSpec((tm, q_lora), row)]
    out_shape = [jax.ShapeDtypeStruct((m, kv_lora + rope), F32), jax.ShapeDtypeStruct((m, q_lora), BF16)]
    if with_keys:
        out_specs += [pl.BlockSpec((tm, kv_lora), row), pl.BlockSpec((kv_lora + rope, tm), lambda i: (0, i))]
        out_shape += [jax.ShapeDtypeStruct((m, kv_lora), BF16), jax.ShapeDtypeStruct((kv_lora + rope, m), BF16)]
    return pl.pallas_call(
        functools.partial(_mla_in_kernel, q_lora=q_lora, kv_lora=kv_lora, rope=rope),
        grid=(m // tm,),
        in_specs=[
            pl.BlockSpec((tm, d), row),
            _const_spec((1, d)),
            _const_spec((d, n)),
            _const_spec((1, q_lora)),
            _const_spec((1, kv_lora)),
            pl.BlockSpec((tm, 2 * rope), row),
        ],
        out_specs=out_specs,
        out_shape=out_shape,
        compiler_params=_params(1),
        name="mla_in",
    )(x, g, w_in_ext, q_g, kv_g, tab)


def _mla_q_kernel(cq_ref, wuq_ref, wuk_ref, tab_ref, qlat_ref, qpe_ref, *, heads, nope, rope, scale):
    q = _dot(cq_ref[...], wuq_ref[...])
    tab = tab_ref[...]
    for h in range(heads):
        qn = q[:, h * nope:(h + 1) * nope].astype(BF16)
        qlat_ref[h] = (_dot(qn, wuk_ref[h]) * scale).astype(qlat_ref.dtype)
        lo = heads * nope + h * 2 * rope
        t = q[:, lo:lo + 2 * rope] * tab
        qpe_ref[h] = ((t + pltpu.roll(t, rope, axis=1))[:, :rope] * scale).astype(qpe_ref.dtype)


def _mla_q(cq, w_uq_ext, w_uk_t, tab, *, heads, nope, rope, scale, out_dtype):
    m, q_lora = cq.shape
    kv_lora = w_uk_t.shape[2]
    tm = _tile(m, 256)
    return pl.pallas_call(
        functools.partial(_mla_q_kernel, heads=heads, nope=nope, rope=rope, scale=scale),
        grid=(m // tm,),
        in_specs=[
            pl.BlockSpec((tm, q_lora), lambda i: (i, 0)),
            _const_spec(w_uq_ext.shape),
            _const_spec(w_uk_t.shape),
            pl.BlockSpec((tm, 2 * rope), lambda i: (i, 0)),
        ],
        out_specs=[
            pl.BlockSpec((heads, tm, kv_lora), lambda i: (0, i, 0)),
            pl.BlockSpec((heads, tm, rope), lambda i: (0, i, 0)),
        ],
        out_shape=[
            jax.ShapeDtypeStruct((heads, m, kv_lora), out_dtype),
            jax.ShapeDtypeStruct((heads, m, rope), out_dtype),
        ],
        compiler_params=_params(1),
        name="mla_q",
    )(cq, w_uq_ext, w_uk_t, tab)


def _mla_prompt_attn_kernel(qlat_ref, qpe_ref, kt_ref, ckv_ref, o_ref, m_sc, l_sc, acc_sc, *, tq, tk, heads, group):
    i = pl.program_id(1)
    j = pl.program_id(2)
    nk = pl.num_programs(2)
    kv_lora = ckv_ref.shape[1]
    rope = qpe_ref.shape[2]
    last_tile = ((i + 1) * tq - 1) // tk

    @pl.when(j == 0)
    def _():
        m_sc[...] = jnp.full(m_sc.shape, NEG_INF, F32)
        l_sc[...] = jnp.zeros(l_sc.shape, F32)
        acc_sc[...] = jnp.zeros(acc_sc.shape, F32)

    @pl.when(j <= last_tile)
    def _():
        qpos = i * tq + lax.broadcasted_iota(jnp.int32, (tq, tk), 0)
        kpos = j * tk + lax.broadcasted_iota(jnp.int32, (tq, tk), 1)
        bias = jnp.where(kpos <= qpos, 0.0, NEG_INF)
        ckv = ckv_ref[...]
        kt_c = kt_ref[:kv_lora, :]
        kt_p = kt_ref[kv_lora:, :]
        def logits(g):
            hs = slice(g * group, (g + 1) * group)
            q = qlat_ref[hs].reshape(group * tq, kv_lora)
            qp = qpe_ref[hs].reshape(group * tq, rope)
            return _dot(q, kt_c) + _dot(qp, kt_p)

        n_groups = heads // group
        s_next = logits(0)
        for g in range(n_groups):
            s = s_next
            if g + 1 < n_groups:
                s_next = logits(g + 1)
            s = (s.reshape(group, tq, tk) + bias[None]).reshape(group * tq, tk)
            m_prev = m_sc[g]
            m_new = jnp.maximum(m_prev, s.max(axis=1, keepdims=True))
            alpha = jnp.exp2(m_prev - m_new)
            p = jnp.exp2(s - m_new)
            l_sc[g] = alpha * l_sc[g] + p.sum(axis=1, keepdims=True)
            acc_sc[g] = alpha * acc_sc[g] + _dot(p.astype(BF16), ckv)
            m_sc[g] = m_new

    @pl.when(j == nk - 1)
    def _():
        for g in range(heads // group):
            o = acc_sc[g] / l_sc[g]
            o_ref[g * group:(g + 1) * group] = o.reshape(group, tq, kv_lora).astype(o_ref.dtype)


def _mla_prompt_attn(qlat, qpe, kt, ckv, *, batch, seq):
    heads, m, kv_lora = qlat.shape
    rope = qpe.shape[2]
    tq = _tile(seq, 128)
    tk = _tile(seq, 512)
    nq, nk = seq // tq, seq // tk
    group = 4
    while heads % group:
        group -= 1

    def key_tile(b, i, j):
        return b * nk + jnp.minimum(j, ((i + 1) * tq - 1) // tk)

    qmap = lambda b, i, j: (0, b * nq + i, 0)
    return pl.pallas_call(
        functools.partial(_mla_prompt_attn_kernel, tq=tq, tk=tk, heads=heads, group=group),
        grid=(batch, nq, nk),
        in_specs=[
            pl.BlockSpec((heads, tq, kv_lora), qmap),
            pl.BlockSpec((heads, tq, rope), qmap),
            pl.BlockSpec((kv_lora + rope, tk), lambda b, i, j: (0, key_tile(b, i, j))),
            pl.BlockSpec((tk, kv_lora), lambda b, i, j: (key_tile(b, i, j), 0)),
        ],
        out_specs=pl.BlockSpec((heads, tq, kv_lora), qmap),
        out_shape=jax.ShapeDtypeStruct((heads, m, kv_lora), BF16),
        scratch_shapes=[
            pltpu.VMEM((heads // group, group * tq, 1), F32),
            pltpu.VMEM((heads // group, group * tq, 1), F32),
            pltpu.VMEM((heads // group, group * tq, kv_lora), F32),
        ],
        compiler_params=_params(3),
        name="mla_prompt_attn",
    )(qlat, qpe, kt, ckv)


def _mla_sample_attn_kernel(pt_ref, qlat_ref, qpe_ref, new_ref, *rest, pg, heads, tq, page):
    del pt_ref
    page_refs = rest[:pg]
    o_ref, kt_sc, m_sc, l_sc, acc_sc = rest[pg:]
    j = pl.program_id(1)
    nj = pl.num_programs(1)
    kv_lora = qlat_ref.shape[2]
    rope = qpe_ref.shape[2]
    rows = heads * tq

    @pl.when(j == 0)
    def _():
        m_sc[...] = jnp.full(m_sc.shape, NEG_INF, F32)
        l_sc[...] = jnp.zeros(l_sc.shape, F32)
        acc_sc[...] = jnp.zeros(acc_sc.shape, F32)

    q = qlat_ref[...].reshape(rows, kv_lora).astype(BF16)
    qp = qpe_ref[...].reshape(rows, rope).astype(BF16)

    def update(s, pv):
        m_prev = m_sc[...]
        m_new = jnp.maximum(m_prev, s.max(axis=1, keepdims=True))
        alpha = jnp.exp2(m_prev - m_new)
        p = jnp.exp2(s - m_new)
        l_sc[...] = alpha * l_sc[...] + p.sum(axis=1, keepdims=True)
        acc_sc[...] = alpha * acc_sc[...] + pv(p.astype(BF16))
        m_sc[...] = m_new

    for k, r in enumerate(page_refs):
        kt_sc[:, k * page:(k + 1) * page] = r[...].astype(BF16)
    ckv_t = kt_sc[:kv_lora, :]
    update(_dot(q, ckv_t) + _dot(qp, kt_sc[kv_lora:, :]), lambda p: _dot_nt(p, ckv_t))

    @pl.when(j == nj - 1)
    def _():
        k_new = jnp.concatenate([new_ref[...], jnp.zeros((page - tq, kv_lora + rope), F32)], axis=0)
        ckv = k_new[:, :kv_lora].astype(BF16)
        s = _dot_nt(q, ckv) + _dot_nt(qp, k_new[:, kv_lora:].astype(BF16))
        t_q = lax.broadcasted_iota(jnp.int32, (tq, page), 0)
        t_k = lax.broadcasted_iota(jnp.int32, (tq, page), 1)
        s = jnp.where((t_k <= t_q)[None], s.reshape(heads, tq, page), NEG_INF).reshape(rows, page)
        update(s, lambda p: _dot(p, ckv))
        o = acc_sc[...] / l_sc[...]
        o_ref[...] = o.reshape(heads, tq, kv_lora).astype(o_ref.dtype)


def _mla_sample_attn(qlat, qpe, rows_new, cache_t, page_table, *, layer, tq, pages_per_step):
    heads, m, kv_lora = qlat.shape
    rope = qpe.shape[2]
    n_seq, n_pages = page_table.shape
    page = cache_t.shape[3]
    pg = pages_per_step
    while n_pages % pg:
        pg -= 1
    qmap = lambda s, j, pt: (0, s, 0)

    def page_spec(k):
        return pl.BlockSpec((None, None, kv_lora + rope, page), lambda s, j, pt: (layer, pt[s * n_pages + j * pg + k], 0, 0))

    grid_spec = pltpu.PrefetchScalarGridSpec(
        num_scalar_prefetch=1,
        grid=(n_seq, n_pages // pg),
        in_specs=[
            pl.BlockSpec((heads, tq, kv_lora), qmap),
            pl.BlockSpec((heads, tq, rope), qmap),
            pl.BlockSpec((tq, kv_lora + rope), lambda s, j, pt: (s, 0)),
        ] + [page_spec(k) for k in range(pg)],
        out_specs=pl.BlockSpec((heads, tq, kv_lora), qmap),
        scratch_shapes=[
            pltpu.VMEM((kv_lora + rope, pg * page), BF16),
            pltpu.VMEM((heads * tq, 1), F32),
            pltpu.VMEM((heads * tq, 1), F32),
            pltpu.VMEM((heads * tq, kv_lora), F32),
        ],
    )
    return pl.pallas_call(
        functools.partial(_mla_sample_attn_kernel, pg=pg, heads=heads, tq=tq, page=page),
        grid_spec=grid_spec,
        out_shape=jax.ShapeDtypeStruct((heads, m, kv_lora), F32),
        compiler_params=_params(2),
        name="mla_sample_attn",
    )(page_table.reshape(-1), qlat, qpe, rows_new, *([cache_t] * pg))


def _mla_out_kernel(o_ref, wuv_ref, wo_ref, x_ref, g_ref, out_ref, *, heads):
    v = jnp.concatenate([_dot(o_ref[h].astype(BF16), wuv_ref[h]).astype(BF16) for h in range(heads)], axis=1)
    out_ref[...] = x_ref[...] + _rms(_dot(v, wo_ref[...]), g_ref[...])


def _mla_out(o_lat, w_uv_t, w_o, x, g):
    heads, m, kv_lora = o_lat.shape
    d = x.shape[1]
    tm = _tile(m, 256)
    return pl.pallas_call(
        functools.partial(_mla_out_kernel, heads=heads),
        grid=(m // tm,),
        in_specs=[
            pl.BlockSpec((heads, tm, kv_lora), lambda i: (0, i, 0)),
            _const_spec(w_uv_t.shape),
            _const_spec(w_o.shape),
            pl.BlockSpec((tm, d), lambda i: (i, 0)),
            _const_spec((1, d)),
        ],
        out_specs=pl.BlockSpec((tm, d), lambda i: (i, 0)),
        out_shape=jax.ShapeDtypeStruct((m, d), F32),
        compiler_params=_params(1),
        name="mla_out",
    )(o_lat, w_uv_t, w_o, x, g)


def _out_proj_kernel(a_ref, w_ref, b_ref, x_ref, g_ref, out_ref):
    y = _dot(a_ref[...].astype(BF16), w_ref[...]) + b_ref[...]
    out_ref[...] = x_ref[...] + _rms(y, g_ref[...])


def _out_proj(a, w, b, x, g):
    m, k = a.shape
    d = x.shape[1]
    tm = _tile(m, 512)
    return pl.pallas_call(
        _out_proj_kernel,
        grid=(m // tm,),
        in_specs=[
            pl.BlockSpec((tm, k), lambda i: (i, 0)),
            _const_spec(w.shape),
            _const_spec((1, d)),
            pl.BlockSpec((tm, d), lambda i: (i, 0)),
            _const_spec((1, d)),
        ],
        out_specs=pl.BlockSpec((tm, d), lambda i: (i, 0)),
        out_shape=jax.ShapeDtypeStruct((m, d), F32),
        compiler_params=_params(1),
        name="out_proj",
    )(a, w, b, x, g)


def _norm_matmul_kernel(*refs, has_bias, has_rope, rot_half, scale):
    x_ref, g_ref, w_ref = refs[:3]
    refs = refs[3:]
    if has_bias:
        b_ref, refs = refs[0], refs[1:]
    if has_rope:
        (c_ref, s1_ref, s2_ref), refs = refs[:3], refs[3:]
    out_ref, h_sc = refs

    @pl.when(pl.program_id(1) == 0)
    def _():
        h_sc[...] = _rms(x_ref[...], g_ref[...]).astype(BF16)

    y = _dot(h_sc[...], w_ref[...])
    if has_bias:
        y = y + b_ref[...]
    if has_rope:
        c, s1, s2 = c_ref[...], s1_ref[...], s2_ref[...]
        for k in range(y.shape[1] // LANES):
            yc = y[:, k * LANES:(k + 1) * LANES]
            yc = yc * c + pltpu.roll(yc, LANES - rot_half, axis=1) * s1 + pltpu.roll(yc, rot_half, axis=1) * s2
            out_ref[:, k * LANES:(k + 1) * LANES] = (yc * scale).astype(out_ref.dtype)
    else:
        out_ref[...] = (y * scale).astype(out_ref.dtype) if scale != 1.0 else y.astype(out_ref.dtype)


def _norm_matmul(x, g, w, b=None, rope_tabs=None, *, rot_half=0, scale=1.0, out_dtype=F32, tm_pref=1024, tn_pref=1024):
    m, d = x.shape
    n = w.shape[1]
    tm = _tile(m, tm_pref)
    tn = min(n, tn_pref)
    while n % tn or tn % LANES:
        tn -= LANES
    in_specs = [
        pl.BlockSpec((tm, d), lambda i, j: (i, 0)),
        _const_spec((1, d)),
        pl.BlockSpec((d, tn), lambda i, j: (0, j)),
    ]
    args = [x, g, w]
    if b is not None:
        in_specs.append(pl.BlockSpec((1, tn), lambda i, j: (0, j)))
        args.append(b)
    if rope_tabs is not None:
        in_specs += [pl.BlockSpec((tm, LANES), lambda i, j: (i, 0))] * 3
        args += list(rope_tabs)
    return pl.pallas_call(
        functools.partial(_norm_matmul_kernel, has_bias=b is not None, has_rope=rope_tabs is not None, rot_half=rot_half, scale=scale),
        grid=(m // tm, n // tn),
        in_specs=in_specs,
        out_specs=pl.BlockSpec((tm, tn), lambda i, j: (i, j)),
        out_shape=jax.ShapeDtypeStruct((m, n), out_dtype),
        scratch_shapes=[pltpu.VMEM((tm, d), BF16)],
        compiler_params=_params(2),
        name="norm_matmul",
    )(*args)


def _ffn_kernel(*refs, per_seq_rows, tiles_per_seq):
    if tiles_per_seq:
        x_ref, gpre_ref, wg_ref, wu_ref, cw_ref, cb_ref, wd_ref, gpost_ref, out_ref, tail_ref, h_sc, acc_sc, carry_sc = refs
    else:
        x_ref, gpre_ref, wg_ref, wu_ref, prev_ref, cw_ref, cb_ref, wd_ref, gpost_ref, out_ref, tail_ref, h_sc, acc_sc = refs
    i = pl.program_id(0)
    f = pl.program_id(1)
    nf = pl.num_programs(1)

    @pl.when(f == 0)
    def _():
        h_sc[...] = _rms(x_ref[...], gpre_ref[...]).astype(BF16)
        acc_sc[...] = jnp.zeros(acc_sc.shape, F32)

    tm = h_sc.shape[0]
    unit = SUBLANES if tiles_per_seq else per_seq_rows
    half = tm // 2 if tm % (2 * unit) == 0 else tm
    blocks = [slice(k * half, (k + 1) * half) for k in range(tm // half)]
    gates, ups = [], []
    for rows in blocks:
        hb = h_sc[rows, :]
        gates.append(_dot(hb, wg_ref[...]))
        ups.append(_dot(hb, wu_ref[...]))
    last8 = gates[-1][half - SUBLANES:, :]
    if tiles_per_seq:
        tail_ref[...] = last8

        @pl.when(i % tiles_per_seq == 0)
        def _():
            carry_sc[f] = jnp.zeros(last8.shape, F32)

        before = carry_sc[f]
        carry_sc[f] = last8
    for k, rows in enumerate(blocks):
        gate, up = gates[k], ups[k]
        r = lax.broadcasted_iota(jnp.int32, gate.shape, 0)
        d1 = pltpu.roll(gate, 1, axis=0)
        d2 = pltpu.roll(gate, 2, axis=0)
        if tiles_per_seq:
            h6 = before[SUBLANES - 2:SUBLANES - 1, :]
            h7 = before[SUBLANES - 1:SUBLANES, :]
            g1 = jnp.where(r == 0, h7, d1)
            g2 = jnp.where(r == 0, h6, jnp.where(r == 1, h7, d2))
            before = gate[half - SUBLANES:, :]
        else:
            tail_ref[rows, :] = gate
            t = r % per_seq_rows
            p2 = prev_ref[rows, :]
            g1 = jnp.where(t == 0, pltpu.roll(p2, half - 1, axis=0), d1)
            g2 = jnp.where(t < 2, p2, d2)
        conv = cb_ref[...] + g2 * cw_ref[0:1, :] + g1 * cw_ref[1:2, :] + gate * cw_ref[2:3, :]
        act = conv * (1.0 / (1.0 + jnp.exp(-conv))) * up
        acc_sc[rows, :] += _dot(act.astype(BF16), wd_ref[...])

    @pl.when(f == nf - 1)
    def _():
        out_ref[...] = x_ref[...] + _rms(acc_sc[...], gpost_ref[...])


def _ffn(x, g_pre, w_up, prev, conv_w, conv_b, w_down, g_post, *, seq_rows):
    m, d = x.shape
    d_ff = w_down.shape[0]
    tf = _tile(d_ff, 512)
    nf = d_ff // tf
    tm = _tile(seq_rows if prev is None else m, 512)
    in_specs = [
        pl.BlockSpec((tm, d), lambda i, f: (i, 0)),
        _const_spec((1, d)),
        pl.BlockSpec((d, tf), lambda i, f: (0, f)),
        pl.BlockSpec((d, tf), lambda i, f: (0, nf + f)),
    ]
    args = [x, g_pre, w_up, w_up]
    scratch = [pltpu.VMEM((tm, d), BF16), pltpu.VMEM((tm, d), F32)]
    if prev is None:
        tiles_per_seq = seq_rows // tm
        scratch.append(pltpu.VMEM((nf, SUBLANES, tf), F32))
        tail_spec = pl.BlockSpec((None, SUBLANES, tf), lambda i, f: (i, 0, f))
        tail_shape = jax.ShapeDtypeStruct((m // tm, SUBLANES, d_ff), F32)
    else:
        assert tm % seq_rows == 0 and seq_rows >= 2
        tiles_per_seq = 0
        in_specs.append(pl.BlockSpec((tm, tf), lambda i, f: (i, f)))
        args.append(prev)
        tail_spec = pl.BlockSpec((tm, tf), lambda i, f: (i, f))
        tail_shape = jax.ShapeDtypeStruct((m, d_ff), F32)
    in_specs += [
        pl.BlockSpec((conv_w.shape[0], tf), lambda i, f: (0, f)),
        pl.BlockSpec((1, tf), lambda i, f: (0, f)),
        pl.BlockSpec((tf, d), lambda i, f: (f, 0)),
        _const_spec((1, d)),
    ]
    args += [conv_w, conv_b, w_down, g_post]
    return pl.pallas_call(
        functools.partial(_ffn_kernel, per_seq_rows=seq_rows, tiles_per_seq=tiles_per_seq),
        grid=(m // tm, nf),
        in_specs=in_specs,
        out_specs=[pl.BlockSpec((tm, d), lambda i, f: (i, 0)), tail_spec],
        out_shape=[jax.ShapeDtypeStruct((m, d), F32), tail_shape],
        scratch_shapes=scratch,
        compiler_params=_params(2),
        name="ffn",
    )(*args)


def _sink_softmax_pv(s_list, v_list, sink):
    m = sink
    for s in s_list:
        m = jnp.maximum(m, s.max(axis=1, keepdims=True))
    denom = jnp.exp(sink - m)
    es = []
    for s in s_list:
        e = jnp.exp(s - m)
        es.append(e)
        denom = denom + e.sum(axis=1, keepdims=True)
    o = None
    for e, v in zip(es, v_list):
        pv = _dot((e / denom).astype(BF16), v)
        o = pv if o is None else o + pv
    return o


def _swa_prompt_kernel(sink_ref, q_ref, kc_ref, kp_ref, vc_ref, vp_ref, o_ref, *, kv_heads, group, hd):
    i = pl.program_id(1)
    w = q_ref.shape[0]
    qi = lax.broadcasted_iota(jnp.int32, (w, 2 * w), 0) + w
    ki = lax.broadcasted_iota(jnp.int32, (w, 2 * w), 1)
    dist = qi - ki
    mask = (dist >= 0) & (dist < w) & ((ki >= w) | (i > 0))
    for kv in range(kv_heads):
        sl = slice(kv * hd, (kv + 1) * hd)
        kk = jnp.concatenate([kp_ref[:, sl], kc_ref[:, sl]], axis=0).astype(BF16)
        vv = jnp.concatenate([vp_ref[:, sl], vc_ref[:, sl]], axis=0).astype(BF16)
        outs = []
        for g in range(group):
            h = kv * group + g
            s = jnp.where(mask, _dot_nt(q_ref[:, h * hd:(h + 1) * hd], kk), NEG_INF)
            outs.append(_sink_softmax_pv([s], [vv], sink_ref[h]))
        o_ref[:, kv * group * hd:(kv + 1) * group * hd] = jnp.concatenate(outs, axis=1).astype(o_ref.dtype)


def _swa_prompt_attn(q, k, v, sinks, *, batch, seq, window, kv_heads, hd):
    m, dq = q.shape
    dk = k.shape[1]
    nb = seq // window
    group = dq // (kv_heads * hd)
    cur = lambda b, i, sk: (b * nb + i, 0)
    prev = lambda b, i, sk: (jnp.maximum(b * nb + i - 1, 0), 0)
    grid_spec = pltpu.PrefetchScalarGridSpec(
        num_scalar_prefetch=1,
        grid=(batch, nb),
        in_specs=[
            pl.BlockSpec((window, dq), cur),
            pl.BlockSpec((window, dk), cur),
            pl.BlockSpec((window, dk), prev),
            pl.BlockSpec((window, dk), cur),
            pl.BlockSpec((window, dk), prev),
        ],
        out_specs=pl.BlockSpec((window, dq), cur),
    )
    return pl.pallas_call(
        functools.partial(_swa_prompt_kernel, kv_heads=kv_heads, group=group, hd=hd),
        grid_spec=grid_spec,
        out_shape=jax.ShapeDtypeStruct((m, dq), BF16),
        compiler_params=_params(2),
        name="swa_prompt_attn",
    )(sinks.reshape(-1), q, k, k, v, v)


def _swa_sample_kernel(sink_ref, q_ref, st_ref, kn_ref, vn_ref, o_ref, *, kv_heads, group, hd):
    tq = q_ref.shape[0]
    w = st_ref.shape[0]
    dk = kv_heads * hd
    rows = group * tq
    t_q = lax.broadcasted_iota(jnp.int32, (rows, w), 0) % tq
    col = lax.broadcasted_iota(jnp.int32, (rows, w), 1)
    grp = lax.broadcasted_iota(jnp.int32, (rows, 1), 0) // tq
    pad = jnp.zeros((w - tq, hd), F32)
    outs = []
    for kv in range(kv_heads):
        sl = slice(kv * hd, (kv + 1) * hd)
        k_old = st_ref[:, sl].astype(BF16)
        v_old = st_ref[:, dk + kv * hd:dk + (kv + 1) * hd].astype(BF16)
        k_new = jnp.concatenate([kn_ref[:, sl], pad], axis=0).astype(BF16)
        v_new = jnp.concatenate([vn_ref[:, sl], pad], axis=0).astype(BF16)
        qs = jnp.concatenate([q_ref[:, (kv * group + g) * hd:(kv * group + g + 1) * hd] for g in range(group)], axis=0).astype(BF16)
        sink = jnp.zeros((rows, 1), F32)
        for g in range(group):
            sink = jnp.where(grp == g, sink_ref[kv * group + g], sink)
        s_old = jnp.where(col > t_q, _dot_nt(qs, k_old), NEG_INF)
        s_new = jnp.where(col <= t_q, _dot_nt(qs, k_new), NEG_INF)
        o = _sink_softmax_pv([s_old, s_new], [v_old, v_new], sink)
        outs += [o[g * tq:(g + 1) * tq] for g in range(group)]
    o_ref[...] = jnp.concatenate(outs, axis=1).astype(o_ref.dtype)


def _swa_sample_attn(q, state, k_new, v_new, sinks, *, tq, kv_heads, hd):
    m, dq = q.shape
    n_seq, window, dkv = state.shape
    dk = k_new.shape[1]
    group = dq // (kv_heads * hd)
    row = lambda s, sk: (s, 0)
    grid_spec = pltpu.PrefetchScalarGridSpec(
        num_scalar_prefetch=1,
        grid=(n_seq,),
        in_specs=[
            pl.BlockSpec((tq, dq), row),
            pl.BlockSpec((None, window, dkv), lambda s, sk: (s, 0, 0)),
            pl.BlockSpec((tq, dk), row),
            pl.BlockSpec((tq, dk), row),
        ],
        out_specs=pl.BlockSpec((tq, dq), row),
    )
    return pl.pallas_call(
        functools.partial(_swa_sample_kernel, kv_heads=kv_heads, group=group, hd=hd),
        grid_spec=grid_spec,
        out_shape=jax.ShapeDtypeStruct((m, dq), F32),
        compiler_params=_params(1),
        name="swa_sample_attn",
    )(sinks.reshape(-1), q, state, k_new, v_new)


def _rope_tables(pos, dim, theta):
    inv = theta ** (-jnp.arange(0, dim, 2, dtype=F32) / dim)
    ang = pos.astype(F32)[:, None] * inv[None, :]
    return jnp.cos(ang), jnp.sin(ang)


def _rot_cols(w, half):
    return jnp.concatenate([-w[..., half:], w[..., :half]], axis=-1)


def kernel(x_prompt, x_sample, cache_mla, state_swa_kv, state_ffn, page_table, ln_mix_pre, ln_mix_post, ln_ffn_pre, ln_ffn_post, w_ffn_up, ffn_conv_w, ffn_conv_b, w_ffn_down, w_mla_in, mla_q_norm, mla_kv_norm, w_mla_uq, w_mla_uk, w_mla_uv, w_mla_o, swa_kv_norm, w_swa_kv, b_swa_kv, w_swa_q, b_swa_q, swa_sinks, w_swa_o, b_swa_o):
    batch, seq, d = x_prompt.shape
    n_seq, tq_s, _ = x_sample.shape
    depth = ln_mix_pre.shape[0]
    n_a = w_mla_in.shape[0]
    kv_lora, heads, nope = w_mla_uk.shape[1:]
    v_dim = w_mla_uv.shape[3]
    q_lora = mla_q_norm.shape[1]
    rope = w_mla_in.shape[2] - q_lora - kv_lora
    page = cache_mla.shape[2]
    past_len = page_table.shape[1] * page
    window, _, kv_heads, hd = state_swa_kv.shape[1:]
    dk = kv_heads * hd
    rot_half = hd // 8
    d_ff = w_ffn_down.shape[1]
    mla_scale = float((nope + rope) ** -0.5)
    swa_scale = float(hd ** -0.5)
    assert 2 * rope == LANES and 2 * hd == LANES and tq_s == SUBLANES and seq % window == 0

    xs = {"p": x_prompt.reshape(batch * seq, d), "s": x_sample.reshape(n_seq * tq_s, d)}
    pos = {"p": jnp.tile(jnp.arange(seq), batch), "s": jnp.tile(past_len + jnp.arange(tq_s), n_seq)}
    mla_tab, swa_tabs = {}, {}
    for st in ("p", "s"):
        cos, sin = _rope_tables(pos[st], rope, MLA_THETA)
        mla_tab[st] = jnp.concatenate([cos, cos, sin, sin], axis=1)
        cos, sin = _rope_tables(pos[st], 2 * rot_half, ROPE_THETA)
        one = jnp.ones((cos.shape[0], hd - 2 * rot_half), F32)
        zero = jnp.zeros_like(one)
        z8 = jnp.zeros_like(sin)
        swa_tabs[st] = tuple(
            jnp.tile(jnp.concatenate(parts, axis=1), (1, LANES // hd))
            for parts in ([cos, cos, one], [-sin, z8, zero], [z8, sin, zero])
        )
    row2 = lambda a: a.reshape(1, -1)

    mla_rows = {"p": [], "s": []}
    ffn_state = {"p": [], "s": []}
    k_sh, v_sh = {}, {}
    swa_state = state_swa_kv.reshape(n_seq, window, 2 * dk)
    cache_t = jnp.swapaxes(cache_mla, 2, 3)
    for l in range(depth):
        if l < n_a:
            a = l
            w_in = w_mla_in[a]
            w_pe = w_in[:, q_lora + kv_lora:]
            w_in_ext = jnp.concatenate([w_in, _rot_cols(w_pe, rope // 2)], axis=1).astype(BF16)
            w_uq = w_mla_uq[a].reshape(q_lora, heads, nope + rope)
            w_uq_pe = w_uq[..., nope:]
            w_uq_ext = jnp.concatenate(
                [w_uq[..., :nope].reshape(q_lora, heads * nope),
                 jnp.concatenate([w_uq_pe, _rot_cols(w_uq_pe, rope // 2)], axis=-1).reshape(q_lora, heads * 2 * rope)],
                axis=1).astype(BF16)
            w_uk_t = jnp.transpose(w_mla_uk[a], (1, 2, 0)).astype(BF16)
            w_uv_t = jnp.transpose(w_mla_uv[a], (1, 0, 2)).astype(BF16)
            w_o = w_mla_o[a].astype(BF16)
            for st in ("p", "s"):
                rows, cq, *keys = _mla_in(xs[st], row2(ln_mix_pre[l]), w_in_ext, row2(mla_q_norm[a]), row2(mla_kv_norm[a]),
                                          mla_tab[st], q_lora=q_lora, kv_lora=kv_lora, rope=rope, with_keys=st == "p")
                mla_rows[st].append(rows)
                qlat, qpe = _mla_q(cq, w_uq_ext, w_uk_t, mla_tab[st], heads=heads, nope=nope, rope=rope, scale=mla_scale * LOG2E,
                                   out_dtype=BF16 if st == "p" else F32)
                if st == "p":
                    ckv, kt = keys
                    o_lat = _mla_prompt_attn(qlat, qpe, kt, ckv, batch=batch, seq=seq)
                else:
                    o_lat = _mla_sample_attn(qlat, qpe, rows, cache_t, page_table, layer=a, tq=tq_s, pages_per_step=MLA_PAGES_PER_STEP)
                xs[st] = _mla_out(o_lat, w_uv_t, w_o, xs[st], row2(ln_mix_post[l]))
        else:
            b = l - n_a
            if l == n_a:
                w_k = w_swa_kv[:, :dk].astype(BF16)
                w_v = w_swa_kv[:, dk:].astype(BF16)
                for st in ("p", "s"):
                    k_sh[st] = _norm_matmul(xs[st], row2(swa_kv_norm), w_k, row2(b_swa_kv[:dk]), swa_tabs[st], rot_half=rot_half)
                    v_sh[st] = _norm_matmul(xs[st], row2(swa_kv_norm), w_v, row2(b_swa_kv[dk:]))
            w_q = w_swa_q[b].astype(BF16)
            w_o = w_swa_o[b].astype(BF16)
            for st in ("p", "s"):
                q = _norm_matmul(xs[st], row2(ln_mix_pre[l]), w_q, row2(b_swa_q[b]), swa_tabs[st], rot_half=rot_half,
                                 scale=swa_scale, out_dtype=BF16 if st == "p" else F32)
                if st == "p":
                    attn = _swa_prompt_attn(q, k_sh[st], v_sh[st], swa_sinks[b], batch=batch, seq=seq, window=window,
                                            kv_heads=kv_heads, hd=hd)
                else:
                    attn = _swa_sample_attn(q, swa_state, k_sh[st], v_sh[st], swa_sinks[b], tq=tq_s, kv_heads=kv_heads, hd=hd)
                xs[st] = _out_proj(attn, w_o, row2(b_swa_o[b]), xs[st], row2(ln_mix_post[l]))
        w_up = w_ffn_up[l].astype(BF16)
        w_down = w_ffn_down[l].astype(BF16)
        for st in ("p", "s"):
            if st == "p":
                prev = None
            else:
                prev = jnp.pad(state_ffn[l], ((0, 0), (0, tq_s - state_ffn.shape[2]), (0, 0))).reshape(n_seq * tq_s, d_ff)
            xs[st], tail = _ffn(xs[st], row2(ln_ffn_pre[l]), w_up, prev, ffn_conv_w[l], row2(ffn_conv_b[l]), w_down,
                                row2(ln_ffn_post[l]), seq_rows=seq if st == "p" else tq_s)
            if st == "p":
                ffn_state[st].append(tail.reshape(batch, -1, SUBLANES, d_ff)[:, -1, SUBLANES - 2:])
            else:
                ffn_state[st].append(tail.reshape(n_seq, tq_s, d_ff)[:, tq_s - 2:])

    kv_p = jnp.stack([k_sh["p"].reshape(batch, seq, kv_heads, hd), v_sh["p"].reshape(batch, seq, kv_heads, hd)], axis=2)
    kv_s_new = jnp.stack([k_sh["s"].reshape(n_seq, tq_s, kv_heads, hd), v_sh["s"].reshape(n_seq, tq_s, kv_heads, hd)], axis=2)
    return (
        xs["p"].reshape(batch, seq, d),
        xs["s"].reshape(n_seq, tq_s, d),
        jnp.stack(mla_rows["p"], axis=0).reshape(n_a, batch, seq, kv_lora + rope),
        jnp.stack(mla_rows["s"], axis=0).reshape(n_a, n_seq, tq_s, kv_lora + rope),
        kv_p[:, seq - window:],
        jnp.concatenate([state_swa_kv, kv_s_new], axis=1)[:, tq_s:],
        jnp.stack(ffn_state["p"], axis=0),
        jnp.stack(ffn_state["s"], axis=0),
    )
```

```python
import functools

import jax
import jax.numpy as jnp
from jax import lax
from jax.experimental import pallas as pl
from jax.experimental.pallas import tpu as pltpu

RMS_EPS = 1e-6
NEG_INF = -1e30
MLA_THETA = 10000.0
ROPE_THETA = 500000.0
LANES = 128
SUBLANES = 8
VMEM_LIMIT_BYTES = 56 * 1024 * 1024
MLA_PAGES_PER_STEP = 32
LOG2E = 1.4426950408889634
BF16 = jnp.bfloat16
F32 = jnp.float32


def _params(n_axes):
    return pltpu.CompilerParams(dimension_semantics=("arbitrary",) * n_axes, vmem_limit_bytes=VMEM_LIMIT_BYTES)


def _const_spec(shape):
    nd = len(shape)
    return pl.BlockSpec(shape, lambda *_: (0,) * nd, pipeline_mode=pl.Buffered(1))


def _tile(m, pref):
    t = min(m, pref)
    while m % t or t % SUBLANES:
        t -= 1
    return t


def _dot(a, b):
    return jnp.dot(a, b, preferred_element_type=F32)


def _dot_nt(a, b):
    return lax.dot_general(a, b, (((1,), (1,)), ((), ())), preferred_element_type=F32)


def _rms(x, g):
    return x * lax.rsqrt(jnp.mean(x * x, axis=-1, keepdims=True) + RMS_EPS) * g


def _mla_in_kernel(x_ref, g_ref, w_ref, qg_ref, kvg_ref, tab_ref, rows_ref, cq_ref, *key_refs, q_lora, kv_lora, rope):
    h = _rms(x_ref[...], g_ref[...]).astype(BF16)
    z = _dot(h, w_ref[...])
    cq = _rms(z[:, :q_lora], qg_ref[...])
    ckv = _rms(z[:, q_lora:q_lora + kv_lora], kvg_ref[...])
    t = z[:, q_lora + kv_lora:] * tab_ref[...]
    kpe2 = t + pltpu.roll(t, rope, axis=1)
    cq_ref[...] = cq.astype(BF16)
    rows_ref[:, :kv_lora] = ckv
    rows_ref[:, kv_lora:] = kpe2[:, :rope]
    if key_refs:
        ckv_ref, kt_ref = key_refs
        ckv_ref[...] = ckv.astype(BF16)
        kt_ref[:kv_lora, :] = ckv.T.astype(BF16)
        kt_ref[kv_lora:, :] = kpe2.T[:rope].astype(BF16)


def _mla_in(x, g, w_in_ext, q_g, kv_g, tab, *, q_lora, kv_lora, rope, with_keys):
    m, d = x.shape
    tm = _tile(m, 512)
    n = w_in_ext.shape[1]
    row = lambda i: (i, 0)
    out_specs = [pl.BlockSpec((tm, kv_lora + rope), row), pl.BlockSpec((tm, q_lora), row)]
    out_shape = [jax.ShapeDtypeStruct((m, kv_lora + rope), F32), jax.ShapeDtypeStruct((m, q_lora), BF16)]
    if with_keys:
        out_specs += [pl.BlockSpec((tm, kv_lora), row), pl.BlockSpec((None, kv_lora + rope, tm), lambda i: (i, 0, 0))]
        out_shape += [jax.ShapeDtypeStruct((m, kv_lora), BF16), jax.ShapeDtypeStruct((m // tm, kv_lora + rope, tm), BF16)]
    return pl.pallas_call(
        functools.partial(_mla_in_kernel, q_lora=q_lora, kv_lora=kv_lora, rope=rope),
        grid=(m // tm,),
        in_specs=[
            pl.BlockSpec((tm, d), row),
            _const_spec((1, d)),
            _const_spec((d, n)),
            _const_spec((1, q_lora)),
            _const_spec((1, kv_lora)),
            pl.BlockSpec((tm, 2 * rope), row),
        ],
        out_specs=out_specs,
        out_shape=out_shape,
        compiler_params=_params(1),
        name="mla_in",
    )(x, g, w_in_ext, q_g, kv_g, tab)


def _mla_q_kernel(cq_ref, wuq_ref, wuk_ref, tab_ref, qlat_ref, qpe_ref, *, heads, nope, rope, scale):
    q = _dot(cq_ref[...], wuq_ref[...])
    tab = tab_ref[...]
    for h in range(heads):
        qn = q[:, h * nope:(h + 1) * nope].astype(BF16)
        qlat_ref[h] = (_dot(qn, wuk_ref[h]) * scale).astype(qlat_ref.dtype)
        lo = heads * nope + h * 2 * rope
        t = q[:, lo:lo + 2 * rope] * tab
        qpe_ref[h] = ((t + pltpu.roll(t, rope, axis=1))[:, :rope] * scale).astype(qpe_ref.dtype)


def _mla_q(cq, w_uq_ext, w_uk_t, tab, *, heads, nope, rope, scale, out_dtype):
    m, q_lora = cq.shape
    kv_lora = w_uk_t.shape[2]
    tm = _tile(m, 256)
    return pl.pallas_call(
        functools.partial(_mla_q_kernel, heads=heads, nope=nope, rope=rope, scale=scale),
        grid=(m // tm,),
        in_specs=[
            pl.BlockSpec((tm, q_lora), lambda i: (i, 0)),
            _const_spec(w_uq_ext.shape),
            _const_spec(w_uk_t.shape),
            pl.BlockSpec((tm, 2 * rope), lambda i: (i, 0)),
        ],
        out_specs=[
            pl.BlockSpec((heads, tm, kv_lora), lambda i: (0, i, 0)),
            pl.BlockSpec((heads, tm, rope), lambda i: (0, i, 0)),
        ],
        out_shape=[
            jax.ShapeDtypeStruct((heads, m, kv_lora), out_dtype),
            jax.ShapeDtypeStruct((heads, m, rope), out_dtype),
        ],
        compiler_params=_params(1),
        name="mla_q",
    )(cq, w_uq_ext, w_uk_t, tab)


def _mla_prompt_attn_kernel(qlat_ref, qpe_ref, kt_ref, ckv_ref, o_ref, m_sc, l_sc, acc_sc, *, tq, tk, heads, group):
    i = pl.program_id(1)
    j = pl.program_id(2)
    nk = pl.num_programs(2)
    kv_lora = ckv_ref.shape[1]
    rope = qpe_ref.shape[2]
    last_tile = ((i + 1) * tq - 1) // tk

    @pl.when(j == 0)
    def _():
        m_sc[...] = jnp.full(m_sc.shape, NEG_INF, F32)
        l_sc[...] = jnp.zeros(l_sc.shape, F32)
        acc_sc[...] = jnp.zeros(acc_sc.shape, F32)

    @pl.when(j <= last_tile)
    def _():
        qpos = i * tq + lax.broadcasted_iota(jnp.int32, (tq, tk), 0)
        kpos = j * tk + lax.broadcasted_iota(jnp.int32, (tq, tk), 1)
        bias = jnp.where(kpos <= qpos, 0.0, NEG_INF)
        ckv = ckv_ref[...]
        kt_c = kt_ref[:kv_lora, :]
        kt_p = kt_ref[kv_lora:, :]
        def logits(g):
            hs = slice(g * group, (g + 1) * group)
            q = qlat_ref[hs].reshape(group * tq, kv_lora)
            qp = qpe_ref[hs].reshape(group * tq, rope)
            return _dot(q, kt_c) + _dot(qp, kt_p)

        n_groups = heads // group
        s_next = logits(0)
        for g in range(n_groups):
            s = s_next
            if g + 1 < n_groups:
                s_next = logits(g + 1)
            s = (s.reshape(group, tq, tk) + bias[None]).reshape(group * tq, tk)
            m_prev = m_sc[g]
            m_new = jnp.maximum(m_prev, s.max(axis=1, keepdims=True))
            alpha = jnp.exp2(m_prev - m_new)
            p = jnp.exp2(s - m_new)
            l_sc[g] = alpha * l_sc[g] + p.sum(axis=1, keepdims=True)
            acc_sc[g] = alpha * acc_sc[g] + _dot(p.astype(BF16), ckv)
            m_sc[g] = m_new

    @pl.when(j == nk - 1)
    def _():
        for g in range(heads // group):
            o = acc_sc[g] / l_sc[g]
            o_ref[g * group:(g + 1) * group] = o.reshape(group, tq, kv_lora).astype(o_ref.dtype)


def _mla_prompt_attn(qlat, qpe, kt, ckv, *, batch, seq):
    heads, m, kv_lora = qlat.shape
    rope = qpe.shape[2]
    tq = _tile(seq, 128)
    tk = kt.shape[2]
    assert seq % tk == 0 and tk % tq == 0
    nq, nk = seq // tq, seq // tk
    group = 4
    while heads % group:
        group -= 1

    def key_tile(b, i, j):
        return b * nk + jnp.minimum(j, ((i + 1) * tq - 1) // tk)

    qmap = lambda b, i, j: (0, b * nq + i, 0)
    return pl.pallas_call(
        functools.partial(_mla_prompt_attn_kernel, tq=tq, tk=tk, heads=heads, group=group),
        grid=(batch, nq, nk),
        in_specs=[
            pl.BlockSpec((heads, tq, kv_lora), qmap),
            pl.BlockSpec((heads, tq, rope), qmap),
            pl.BlockSpec((None, kv_lora + rope, tk), lambda b, i, j: (key_tile(b, i, j), 0, 0)),
            pl.BlockSpec((tk, kv_lora), lambda b, i, j: (key_tile(b, i, j), 0)),
        ],
        out_specs=pl.BlockSpec((heads, tq, kv_lora), qmap),
        out_shape=jax.ShapeDtypeStruct((heads, m, kv_lora), BF16),
        scratch_shapes=[
            pltpu.VMEM((heads // group, group * tq, 1), F32),
            pltpu.VMEM((heads // group, group * tq, 1), F32),
            pltpu.VMEM((heads // group, group * tq, kv_lora), F32),
        ],
        compiler_params=_params(3),
        name="mla_prompt_attn",
    )(qlat, qpe, kt, ckv)


def _mla_sample_attn_kernel(pt_ref, qlat_ref, qpe_ref, new_ref, *rest, pg, heads, tq, page):
    del pt_ref
    page_refs = rest[:pg]
    o_ref, kt_sc, m_sc, l_sc, acc_sc = rest[pg:]
    j = pl.program_id(1)
    nj = pl.num_programs(1)
    kv_lora = qlat_ref.shape[2]
    rope = qpe_ref.shape[2]
    rows = heads * tq

    @pl.when(j == 0)
    def _():
        m_sc[...] = jnp.full(m_sc.shape, NEG_INF, F32)
        l_sc[...] = jnp.zeros(l_sc.shape, F32)
        acc_sc[...] = jnp.zeros(acc_sc.shape, F32)

    q = qlat_ref[...].reshape(rows, kv_lora).astype(BF16)
    qp = qpe_ref[...].reshape(rows, rope).astype(BF16)

    def update(s, pv):
        m_prev = m_sc[...]
        m_new = jnp.maximum(m_prev, s.max(axis=1, keepdims=True))
        alpha = jnp.exp2(m_prev - m_new)
        p = jnp.exp2(s - m_new)
        l_sc[...] = alpha * l_sc[...] + p.sum(axis=1, keepdims=True)
        acc_sc[...] = alpha * acc_sc[...] + pv(p.astype(BF16))
        m_sc[...] = m_new

    n_slabs = 2 if pg % 2 == 0 else 1
    per = pg // n_slabs
    logits, values = [], []
    for h in range(n_slabs):
        for k in range(h * per, (h + 1) * per):
            kt_sc[:, k * page:(k + 1) * page] = page_refs[k][...].astype(BF16)
        cols = slice(h * per * page, (h + 1) * per * page)
        ckv_t = kt_sc[:kv_lora, cols]
        logits.append(_dot(q, ckv_t) + _dot(qp, kt_sc[kv_lora:, cols]))
        values.append(ckv_t)
    for s, ckv_t in zip(logits, values):
        update(s, lambda p, v=ckv_t: _dot_nt(p, v))

    @pl.when(j == nj - 1)
    def _():
        k_new = jnp.concatenate([new_ref[...], jnp.zeros((page - tq, kv_lora + rope), F32)], axis=0)
        ckv = k_new[:, :kv_lora].astype(BF16)
        s = _dot_nt(q, ckv) + _dot_nt(qp, k_new[:, kv_lora:].astype(BF16))
        t_q = lax.broadcasted_iota(jnp.int32, (tq, page), 0)
        t_k = lax.broadcasted_iota(jnp.int32, (tq, page), 1)
        s = jnp.where((t_k <= t_q)[None], s.reshape(heads, tq, page), NEG_INF).reshape(rows, page)
        update(s, lambda p: _dot(p, ckv))
        o = acc_sc[...] / l_sc[...]
        o_ref[...] = o.reshape(heads, tq, kv_lora).astype(o_ref.dtype)


def _mla_sample_attn(qlat, qpe, rows_new, cache_t, page_table, *, layer, tq, pages_per_step):
    heads, m, kv_lora = qlat.shape
    rope = qpe.shape[2]
    n_seq, n_pages = page_table.shape
    page = cache_t.shape[3]
    pg = pages_per_step
    while n_pages % pg:
        pg -= 1
    qmap = lambda s, j, pt: (0, s, 0)

    def page_spec(k):
        return pl.BlockSpec((None, None, kv_lora + rope, page), lambda s, j, pt: (layer, pt[s * n_pages + j * pg + k], 0, 0))

    grid_spec = pltpu.PrefetchScalarGridSpec(
        num_scalar_prefetch=1,
        grid=(n_seq, n_pages // pg),
        in_specs=[
            pl.BlockSpec((heads, tq, kv_lora), qmap),
            pl.BlockSpec((heads, tq, rope), qmap),
            pl.BlockSpec((tq, kv_lora + rope), lambda s, j, pt: (s, 0)),
        ] + [page_spec(k) for k in range(pg)],
        out_specs=pl.BlockSpec((heads, tq, kv_lora), qmap),
        scratch_shapes=[
            pltpu.VMEM((kv_lora + rope, pg * page), BF16),
            pltpu.VMEM((heads * tq, 1), F32),
            pltpu.VMEM((heads * tq, 1), F32),
            pltpu.VMEM((heads * tq, kv_lora), F32),
        ],
    )
    return pl.pallas_call(
        functools.partial(_mla_sample_attn_kernel, pg=pg, heads=heads, tq=tq, page=page),
        grid_spec=grid_spec,
        out_shape=jax.ShapeDtypeStruct((heads, m, kv_lora), F32),
        compiler_params=_params(2),
        name="mla_sample_attn",
    )(page_table.reshape(-1), qlat, qpe, rows_new, *([cache_t] * pg))


def _mla_out_kernel(o_ref, wuv_ref, wo_ref, x_ref, g_ref, out_ref, *, heads):
    v = jnp.concatenate([_dot(o_ref[h].astype(BF16), wuv_ref[h]).astype(BF16) for h in range(heads)], axis=1)
    out_ref[...] = x_ref[...] + _rms(_dot(v, wo_ref[...]), g_ref[...])


def _mla_out(o_lat, w_uv_t, w_o, x, g):
    heads, m, kv_lora = o_lat.shape
    d = x.shape[1]
    tm = _tile(m, 256)
    return pl.pallas_call(
        functools.partial(_mla_out_kernel, heads=heads),
        grid=(m // tm,),
        in_specs=[
            pl.BlockSpec((heads, tm, kv_lora), lambda i: (0, i, 0)),
            _const_spec(w_uv_t.shape),
            _const_spec(w_o.shape),
            pl.BlockSpec((tm, d), lambda i: (i, 0)),
            _const_spec((1, d)),
        ],
        out_specs=pl.BlockSpec((tm, d), lambda i: (i, 0)),
        out_shape=jax.ShapeDtypeStruct((m, d), F32),
        compiler_params=_params(1),
        name="mla_out",
    )(o_lat, w_uv_t, w_o, x, g)


def _out_proj_kernel(a_ref, w_ref, b_ref, x_ref, g_ref, out_ref):
    y = _dot(a_ref[...].astype(BF16), w_ref[...]) + b_ref[...]
    out_ref[...] = x_ref[...] + _rms(y, g_ref[...])


def _out_proj(a, w, b, x, g):
    m, k = a.shape
    d = x.shape[1]
    tm = _tile(m, 512)
    return pl.pallas_call(
        _out_proj_kernel,
        grid=(m // tm,),
        in_specs=[
            pl.BlockSpec((tm, k), lambda i: (i, 0)),
            _const_spec(w.shape),
            _const_spec((1, d)),
            pl.BlockSpec((tm, d), lambda i: (i, 0)),
            _const_spec((1, d)),
        ],
        out_specs=pl.BlockSpec((tm, d), lambda i: (i, 0)),
        out_shape=jax.ShapeDtypeStruct((m, d), F32),
        compiler_params=_params(1),
        name="out_proj",
    )(a, w, b, x, g)


def _norm_matmul_kernel(*refs, has_bias, has_rope, rot_half, scale):
    x_ref, g_ref, w_ref = refs[:3]
    refs = refs[3:]
    if has_bias:
        b_ref, refs = refs[0], refs[1:]
    if has_rope:
        (c_ref, s1_ref, s2_ref), refs = refs[:3], refs[3:]
    out_ref, h_sc = refs

    @pl.when(pl.program_id(1) == 0)
    def _():
        h_sc[...] = _rms(x_ref[...], g_ref[...]).astype(BF16)

    y = _dot(h_sc[...], w_ref[...])
    if has_bias:
        y = y + b_ref[...]
    if has_rope:
        c, s1, s2 = c_ref[...], s1_ref[...], s2_ref[...]
        for k in range(y.shape[1] // LANES):
            yc = y[:, k * LANES:(k + 1) * LANES]
            yc = yc * c + pltpu.roll(yc, LANES - rot_half, axis=1) * s1 + pltpu.roll(yc, rot_half, axis=1) * s2
            out_ref[:, k * LANES:(k + 1) * LANES] = (yc * scale).astype(out_ref.dtype)
    else:
        out_ref[...] = (y * scale).astype(out_ref.dtype) if scale != 1.0 else y.astype(out_ref.dtype)


def _norm_matmul(x, g, w, b=None, rope_tabs=None, *, rot_half=0, scale=1.0, out_dtype=F32, tm_pref=1024, tn_pref=1024):
    m, d = x.shape
    n = w.shape[1]
    tm = _tile(m, tm_pref)
    tn = min(n, tn_pref)
    while n % tn or tn % LANES:
        tn -= LANES
    in_specs = [
        pl.BlockSpec((tm, d), lambda i, j: (i, 0)),
        _const_spec((1, d)),
        pl.BlockSpec((d, tn), lambda i, j: (0, j)),
    ]
    args = [x, g, w]
    if b is not None:
        in_specs.append(pl.BlockSpec((1, tn), lambda i, j: (0, j)))
        args.append(b)
    if rope_tabs is not None:
        in_specs += [pl.BlockSpec((tm, LANES), lambda i, j: (i, 0))] * 3
        args += list(rope_tabs)
    return pl.pallas_call(
        functools.partial(_norm_matmul_kernel, has_bias=b is not None, has_rope=rope_tabs is not None, rot_half=rot_half, scale=scale),
        grid=(m // tm, n // tn),
        in_specs=in_specs,
        out_specs=pl.BlockSpec((tm, tn), lambda i, j: (i, j)),
        out_shape=jax.ShapeDtypeStruct((m, n), out_dtype),
        scratch_shapes=[pltpu.VMEM((tm, d), BF16)],
        compiler_params=_params(2),
        name="norm_matmul",
    )(*args)


def _ffn_kernel(*refs, per_seq_rows, tiles_per_seq):
    if tiles_per_seq:
        x_ref, gpre_ref, wg_ref, wu_ref, cw_ref, cb_ref, wd_ref, gpost_ref, out_ref, tail_ref, h_sc, acc_sc, carry_sc = refs
    else:
        x_ref, gpre_ref, wg_ref, wu_ref, prev_ref, cw_ref, cb_ref, wd_ref, gpost_ref, out_ref, tail_ref, h_sc, acc_sc = refs
    i = pl.program_id(0)
    f = pl.program_id(1)
    nf = pl.num_programs(1)

    @pl.when(f == 0)
    def _():
        h_sc[...] = _rms(x_ref[...], gpre_ref[...]).astype(BF16)
        acc_sc[...] = jnp.zeros(acc_sc.shape, F32)

    tm = h_sc.shape[0]
    unit = SUBLANES if tiles_per_seq else per_seq_rows
    half = tm // 2 if tm % (2 * unit) == 0 else tm
    blocks = [slice(k * half, (k + 1) * half) for k in range(tm // half)]
    gates, ups = [], []
    for rows in blocks:
        hb = h_sc[rows, :]
        gates.append(_dot(hb, wg_ref[...]))
        ups.append(_dot(hb, wu_ref[...]))
    last8 = gates[-1][half - SUBLANES:, :]
    if tiles_per_seq:
        tail_ref[...] = last8

        @pl.when(i % tiles_per_seq == 0)
        def _():
            carry_sc[f] = jnp.zeros(last8.shape, F32)

        before = carry_sc[f]
        carry_sc[f] = last8
    for k, rows in enumerate(blocks):
        gate, up = gates[k], ups[k]
        r = lax.broadcasted_iota(jnp.int32, gate.shape, 0)
        d1 = pltpu.roll(gate, 1, axis=0)
        d2 = pltpu.roll(gate, 2, axis=0)
        if tiles_per_seq:
            h6 = before[SUBLANES - 2:SUBLANES - 1, :]
            h7 = before[SUBLANES - 1:SUBLANES, :]
            g1 = jnp.where(r == 0, h7, d1)
            g2 = jnp.where(r == 0, h6, jnp.where(r == 1, h7, d2))
            before = gate[half - SUBLANES:, :]
        else:
            tail_ref[rows, :] = gate
            t = r % per_seq_rows
            p2 = prev_ref[rows, :]
            g1 = jnp.where(t == 0, pltpu.roll(p2, half - 1, axis=0), d1)
            g2 = jnp.where(t < 2, p2, d2)
        conv = cb_ref[...] + g2 * cw_ref[0:1, :] + g1 * cw_ref[1:2, :] + gate * cw_ref[2:3, :]
        act = conv * (1.0 / (1.0 + jnp.exp(-conv))) * up
        acc_sc[rows, :] += _dot(act.astype(BF16), wd_ref[...])

    @pl.when(f == nf - 1)
    def _():
        out_ref[...] = x_ref[...] + _rms(acc_sc[...], gpost_ref[...])


def _ffn(x, g_pre, w_up_blk, prev, conv_w, conv_b, w_down, g_post, *, layer, seq_rows):
    m, d = x.shape
    d_ff = w_down.shape[1]
    tf = w_up_blk.shape[3]
    nf = d_ff // tf
    tm = _tile(seq_rows if prev is None else m, 512)
    in_specs = [
        pl.BlockSpec((tm, d), lambda i, f: (i, 0)),
        _const_spec((1, d)),
        pl.BlockSpec((None, None, d, tf), lambda i, f: (layer, f, 0, 0)),
        pl.BlockSpec((None, None, d, tf), lambda i, f: (layer, nf + f, 0, 0)),
    ]
    args = [x, g_pre, w_up_blk, w_up_blk]
    scratch = [pltpu.VMEM((tm, d), BF16), pltpu.VMEM((tm, d), F32)]
    if prev is None:
        tiles_per_seq = seq_rows // tm
        scratch.append(pltpu.VMEM((nf, SUBLANES, tf), F32))
        tail_spec = pl.BlockSpec((None, SUBLANES, tf), lambda i, f: (i, 0, f))
        tail_shape = jax.ShapeDtypeStruct((m // tm, SUBLANES, d_ff), F32)
    else:
        assert tm % seq_rows == 0 and seq_rows >= 2
        tiles_per_seq = 0
        in_specs.append(pl.BlockSpec((tm, tf), lambda i, f: (i, f)))
        args.append(prev)
        tail_spec = pl.BlockSpec((tm, tf), lambda i, f: (i, f))
        tail_shape = jax.ShapeDtypeStruct((m, d_ff), F32)
    in_specs += [
        pl.BlockSpec((None, conv_w.shape[1], tf), lambda i, f: (layer, 0, f)),
        pl.BlockSpec((None, 1, tf), lambda i, f: (layer, 0, f)),
        pl.BlockSpec((None, tf, d), lambda i, f: (layer, f, 0)),
        _const_spec((1, d)),
    ]
    args += [conv_w, conv_b, w_down, g_post]
    return pl.pallas_call(
        functools.partial(_ffn_kernel, per_seq_rows=seq_rows, tiles_per_seq=tiles_per_seq),
        grid=(m // tm, nf),
        in_specs=in_specs,
        out_specs=[pl.BlockSpec((tm, d), lambda i, f: (i, 0)), tail_spec],
        out_shape=[jax.ShapeDtypeStruct((m, d), F32), tail_shape],
        scratch_shapes=scratch,
        compiler_params=_params(2),
        name="ffn",
    )(*args)


def _sink_softmax(s_list, sink):
    m = sink
    for s in s_list:
        m = jnp.maximum(m, s.max(axis=1, keepdims=True))
    denom = jnp.exp(sink - m)
    es = []
    for s in s_list:
        e = jnp.exp(s - m)
        es.append(e)
        denom = denom + e.sum(axis=1, keepdims=True)
    return [(e / denom).astype(BF16) for e in es]


def _swa_prompt_kernel(sink_ref, q_ref, kc_ref, kp_ref, vc_ref, vp_ref, o_ref, *, kv_heads, group, hd):
    i = pl.program_id(1)
    w = q_ref.shape[0]
    qi = lax.broadcasted_iota(jnp.int32, (w, 2 * w), 0) + w
    ki = lax.broadcasted_iota(jnp.int32, (w, 2 * w), 1)
    dist = qi - ki
    mask = (dist >= 0) & (dist < w) & ((ki >= w) | (i > 0))
    def scores(kv):
        sl = slice(kv * hd, (kv + 1) * hd)
        kk = jnp.concatenate([kp_ref[:, sl], kc_ref[:, sl]], axis=0).astype(BF16)
        return [_dot_nt(q_ref[:, (kv * group + g) * hd:(kv * group + g + 1) * hd], kk) for g in range(group)]

    s_next = scores(0)
    for kv in range(kv_heads):
        s_cur = s_next
        if kv + 1 < kv_heads:
            s_next = scores(kv + 1)
        sl = slice(kv * hd, (kv + 1) * hd)
        vv = jnp.concatenate([vp_ref[:, sl], vc_ref[:, sl]], axis=0).astype(BF16)
        outs = []
        for g in range(group):
            p = _sink_softmax([jnp.where(mask, s_cur[g], NEG_INF)], sink_ref[kv * group + g])[0]
            outs.append(_dot(p, vv))
        o_ref[:, kv * group * hd:(kv + 1) * group * hd] = jnp.concatenate(outs, axis=1).astype(o_ref.dtype)


def _swa_prompt_attn(q, k, v, sinks, *, batch, seq, window, kv_heads, hd):
    m, dq = q.shape
    dk = k.shape[1]
    nb = seq // window
    group = dq // (kv_heads * hd)
    cur = lambda b, i, sk: (b * nb + i, 0)
    prev = lambda b, i, sk: (jnp.maximum(b * nb + i - 1, 0), 0)
    grid_spec = pltpu.PrefetchScalarGridSpec(
        num_scalar_prefetch=1,
        grid=(batch, nb),
        in_specs=[
            pl.BlockSpec((window, dq), cur),
            pl.BlockSpec((window, dk), cur),
            pl.BlockSpec((window, dk), prev),
            pl.BlockSpec((window, dk), cur),
            pl.BlockSpec((window, dk), prev),
        ],
        out_specs=pl.BlockSpec((window, dq), cur),
    )
    return pl.pallas_call(
        functools.partial(_swa_prompt_kernel, kv_heads=kv_heads, group=group, hd=hd),
        grid_spec=grid_spec,
        out_shape=jax.ShapeDtypeStruct((m, dq), BF16),
        compiler_params=_params(2),
        name="swa_prompt_attn",
    )(sinks.reshape(-1), q, k, k, v, v)


def _swa_sample_kernel(sink_ref, q_ref, st_ref, kn_ref, vn_ref, o_ref, *, tq, kv_heads, group, hd):
    n_seq, w, _ = st_ref.shape
    dk = kv_heads * hd
    rows = group * tq
    t_q = lax.broadcasted_iota(jnp.int32, (rows, w), 0) % tq
    col = lax.broadcasted_iota(jnp.int32, (rows, w), 1)
    grp = lax.broadcasted_iota(jnp.int32, (rows, 1), 0) // tq
    pad = jnp.zeros((w - tq, hd), F32)
    sinks = []
    for kv in range(kv_heads):
        sink = jnp.zeros((rows, 1), F32)
        for g in range(group):
            sink = jnp.where(grp == g, sink_ref[kv * group + g], sink)
        sinks.append(sink)
    scores = []
    for b in range(n_seq):
        rs = slice(b * tq, (b + 1) * tq)
        for kv in range(kv_heads):
            sl = slice(kv * hd, (kv + 1) * hd)
            k_old = st_ref[b, :, sl].astype(BF16)
            k_new = jnp.concatenate([kn_ref[rs, sl], pad], axis=0).astype(BF16)
            qs = jnp.concatenate([q_ref[rs, (kv * group + g) * hd:(kv * group + g + 1) * hd] for g in range(group)], axis=0).astype(BF16)
            scores.append([jnp.where(col > t_q, _dot_nt(qs, k_old), NEG_INF), jnp.where(col <= t_q, _dot_nt(qs, k_new), NEG_INF)])
    probs = [_sink_softmax(s, sinks[n % kv_heads]) for n, s in enumerate(scores)]
    for b in range(n_seq):
        rs = slice(b * tq, (b + 1) * tq)
        outs = []
        for kv in range(kv_heads):
            sl = slice(kv * hd, (kv + 1) * hd)
            v_old = st_ref[b, :, dk + kv * hd:dk + (kv + 1) * hd].astype(BF16)
            v_new = jnp.concatenate([vn_ref[rs, sl], pad], axis=0).astype(BF16)
            p_old, p_new = probs[b * kv_heads + kv]
            o = _dot(p_old, v_old) + _dot(p_new, v_new)
            outs += [o[g * tq:(g + 1) * tq] for g in range(group)]
        o_ref[rs, :] = jnp.concatenate(outs, axis=1).astype(o_ref.dtype)


def _swa_sample_attn(q, state, k_new, v_new, sinks, *, tq, kv_heads, hd):
    m, dq = q.shape
    n_seq, window, dkv = state.shape
    dk = k_new.shape[1]
    group = dq // (kv_heads * hd)
    nb = 4
    while n_seq % nb:
        nb -= 1
    row = lambda s, sk: (s, 0)
    grid_spec = pltpu.PrefetchScalarGridSpec(
        num_scalar_prefetch=1,
        grid=(n_seq // nb,),
        in_specs=[
            pl.BlockSpec((nb * tq, dq), row),
            pl.BlockSpec((nb, window, dkv), lambda s, sk: (s, 0, 0)),
            pl.BlockSpec((nb * tq, dk), row),
            pl.BlockSpec((nb * tq, dk), row),
        ],
        out_specs=pl.BlockSpec((nb * tq, dq), row),
    )
    return pl.pallas_call(
        functools.partial(_swa_sample_kernel, tq=tq, kv_heads=kv_heads, group=group, hd=hd),
        grid_spec=grid_spec,
        out_shape=jax.ShapeDtypeStruct((m, dq), F32),
        compiler_params=_params(1),
        name="swa_sample_attn",
    )(sinks.reshape(-1), q, state, k_new, v_new)


def _rope_tables(pos, dim, theta):
    inv = theta ** (-jnp.arange(0, dim, 2, dtype=F32) / dim)
    ang = pos.astype(F32)[:, None] * inv[None, :]
    return jnp.cos(ang), jnp.sin(ang)


def _rot_cols(w, half):
    return jnp.concatenate([-w[..., half:], w[..., :half]], axis=-1)


def kernel(x_prompt, x_sample, cache_mla, state_swa_kv, state_ffn, page_table, ln_mix_pre, ln_mix_post, ln_ffn_pre, ln_ffn_post, w_ffn_up, ffn_conv_w, ffn_conv_b, w_ffn_down, w_mla_in, mla_q_norm, mla_kv_norm, w_mla_uq, w_mla_uk, w_mla_uv, w_mla_o, swa_kv_norm, w_swa_kv, b_swa_kv, w_swa_q, b_swa_q, swa_sinks, w_swa_o, b_swa_o):
    batch, seq, d = x_prompt.shape
    n_seq, tq_s, _ = x_sample.shape
    depth = ln_mix_pre.shape[0]
    n_a = w_mla_in.shape[0]
    kv_lora, heads, nope = w_mla_uk.shape[1:]
    v_dim = w_mla_uv.shape[3]
    q_lora = mla_q_norm.shape[1]
    rope = w_mla_in.shape[2] - q_lora - kv_lora
    page = cache_mla.shape[2]
    past_len = page_table.shape[1] * page
    window, _, kv_heads, hd = state_swa_kv.shape[1:]
    dk = kv_heads * hd
    rot_half = hd // 8
    d_ff = w_ffn_down.shape[1]
    mla_scale = float((nope + rope) ** -0.5)
    swa_scale = float(hd ** -0.5)
    assert 2 * rope == LANES and 2 * hd == LANES and tq_s == SUBLANES and seq % window == 0

    xs = {"p": x_prompt.reshape(batch * seq, d), "s": x_sample.reshape(n_seq * tq_s, d)}
    pos = {"p": jnp.tile(jnp.arange(seq), batch), "s": jnp.tile(past_len + jnp.arange(tq_s), n_seq)}
    mla_tab, swa_tabs = {}, {}
    for st in ("p", "s"):
        cos, sin = _rope_tables(pos[st], rope, MLA_THETA)
        mla_tab[st] = jnp.concatenate([cos, cos, sin, sin], axis=1)
        cos, sin = _rope_tables(pos[st], 2 * rot_half, ROPE_THETA)
        one = jnp.ones((cos.shape[0], hd - 2 * rot_half), F32)
        zero = jnp.zeros_like(one)
        z8 = jnp.zeros_like(sin)
        swa_tabs[st] = tuple(
            jnp.tile(jnp.concatenate(parts, axis=1), (1, LANES // hd))
            for parts in ([cos, cos, one], [-sin, z8, zero], [z8, sin, zero])
        )
    row2 = lambda a: a.reshape(1, -1)

    mla_rows = {"p": [], "s": []}
    ffn_state = {"p": [], "s": []}
    k_sh, v_sh = {}, {}
    swa_state = state_swa_kv.reshape(n_seq, window, 2 * dk)
    cache_t = jnp.swapaxes(cache_mla, 2, 3)
    tf = _tile(d_ff, 512)
    w_up_blk = jnp.transpose(w_ffn_up.astype(BF16).reshape(depth, d, 2 * d_ff // tf, tf), (0, 2, 1, 3))
    w_down_bf = w_ffn_down.astype(BF16)
    conv_b3 = ffn_conv_b.reshape(depth, 1, d_ff)
    for l in range(depth):
        if l < n_a:
            a = l
            w_in = w_mla_in[a]
            w_pe = w_in[:, q_lora + kv_lora:]
            w_in_ext = jnp.concatenate([w_in, _rot_cols(w_pe, rope // 2)], axis=1).astype(BF16)
            w_uq = w_mla_uq[a].reshape(q_lora, heads, nope + rope)
            w_uq_pe = w_uq[..., nope:]
            w_uq_ext = jnp.concatenate(
                [w_uq[..., :nope].reshape(q_lora, heads * nope),
                 jnp.concatenate([w_uq_pe, _rot_cols(w_uq_pe, rope // 2)], axis=-1).reshape(q_lora, heads * 2 * rope)],
                axis=1).astype(BF16)
            w_uk_t = jnp.transpose(w_mla_uk[a], (1, 2, 0)).astype(BF16)
            w_uv_t = jnp.transpose(w_mla_uv[a], (1, 0, 2)).astype(BF16)
            w_o = w_mla_o[a].astype(BF16)
            for st in ("p", "s"):
                rows, cq, *keys = _mla_in(xs[st], row2(ln_mix_pre[l]), w_in_ext, row2(mla_q_norm[a]), row2(mla_kv_norm[a]),
                                          mla_tab[st], q_lora=q_lora, kv_lora=kv_lora, rope=rope, with_keys=st == "p")
                mla_rows[st].append(rows)
                qlat, qpe = _mla_q(cq, w_uq_ext, w_uk_t, mla_tab[st], heads=heads, nope=nope, rope=rope, scale=mla_scale * LOG2E,
                                   out_dtype=BF16 if st == "p" else F32)
                if st == "p":
                    ckv, kt = keys
                    o_lat = _mla_prompt_attn(qlat, qpe, kt, ckv, batch=batch, seq=seq)
                else:
                    o_lat = _mla_sample_attn(qlat, qpe, rows, cache_t, page_table, layer=a, tq=tq_s, pages_per_step=MLA_PAGES_PER_STEP)
                xs[st] = _mla_out(o_lat, w_uv_t, w_o, xs[st], row2(ln_mix_post[l]))
        else:
            b = l - n_a
            if l == n_a:
                w_k = w_swa_kv[:, :dk].astype(BF16)
                w_v = w_swa_kv[:, dk:].astype(BF16)
                for st in ("p", "s"):
                    k_sh[st] = _norm_matmul(xs[st], row2(swa_kv_norm), w_k, row2(b_swa_kv[:dk]), swa_tabs[st], rot_half=rot_half)
                    v_sh[st] = _norm_matmul(xs[st], row2(swa_kv_norm), w_v, row2(b_swa_kv[dk:]))
            w_q = w_swa_q[b].astype(BF16)
            w_o = w_swa_o[b].astype(BF16)
            for st in ("p", "s"):
                q = _norm_matmul(xs[st], row2(ln_mix_pre[l]), w_q, row2(b_swa_q[b]), swa_tabs[st], rot_half=rot_half,
                                 scale=swa_scale, out_dtype=BF16 if st == "p" else F32)
                if st == "p":
                    attn = _swa_prompt_attn(q, k_sh[st], v_sh[st], swa_sinks[b], batch=batch, seq=seq, window=window,
                                            kv_heads=kv_heads, hd=hd)
                else:
                    attn = _swa_sample_attn(q, swa_state, k_sh[st], v_sh[st], swa_sinks[b], tq=tq_s, kv_heads=kv_heads, hd=hd)
                xs[st] = _out_proj(attn, w_o, row2(b_swa_o[b]), xs[st], row2(ln_mix_post[l]))
        for st in ("p", "s"):
            if st == "p":
                prev = None
            else:
                prev = jnp.pad(state_ffn[l], ((0, 0), (0, tq_s - state_ffn.shape[2]), (0, 0))).reshape(n_seq * tq_s, d_ff)
            xs[st], tail = _ffn(xs[st], row2(ln_ffn_pre[l]), w_up_blk, prev, ffn_conv_w, conv_b3, w_down_bf,
                                row2(ln_ffn_post[l]), layer=l, seq_rows=seq if st == "p" else tq_s)
            if st == "p":
                ffn_state[st].append(tail.reshape(batch, -1, SUBLANES, d_ff)[:, -1, SUBLANES - 2:])
            else:
                ffn_state[st].append(tail.reshape(n_seq, tq_s, d_ff)[:, tq_s - 2:])

    kv_p = jnp.stack([a.reshape(batch, seq, kv_heads, hd)[:, seq - window:] for a in (k_sh["p"], v_sh["p"])], axis=2)
    kv_s_new = jnp.stack([k_sh["s"].reshape(n_seq, tq_s, kv_heads, hd), v_sh["s"].reshape(n_seq, tq_s, kv_heads, hd)], axis=2)
    return (
        xs["p"].reshape(batch, seq, d),
        xs["s"].reshape(n_seq, tq_s, d),
        jnp.stack(mla_rows["p"], axis=0).reshape(n_a, batch, seq, kv_lora + rope),
        jnp.stack(mla_rows["s"], axis=0).reshape(n_a, n_seq, tq_s, kv_lora + rope),
        kv_p,
        jnp.concatenate([state_swa_kv, kv_s_new], axis=1)[:, tq_s:],
        jnp.stack(ffn_state["p"], axis=0),
        jnp.stack(ffn_state["s"], axis=0),
    )
```

```python
import functools

import jax
import jax.numpy as jnp
from jax import lax
from jax.experimental import pallas as pl
from jax.experimental.pallas import tpu as pltpu

RMS_EPS = 1e-6
NEG_INF = -1e30
MLA_THETA = 10000.0
ROPE_THETA = 500000.0
LANES = 128
SUBLANES = 8
VMEM_LIMIT_BYTES = 56 * 1024 * 1024
MLA_PAGES_PER_STEP = 32
FFN_ROW_TILE = 1024
FFN_ROW_BLOCK = 256
LOG2E = 1.4426950408889634
BF16 = jnp.bfloat16
F32 = jnp.float32


def _params(n_axes):
    return pltpu.CompilerParams(dimension_semantics=("arbitrary",) * n_axes, vmem_limit_bytes=VMEM_LIMIT_BYTES)


def _const_spec(shape):
    nd = len(shape)
    return pl.BlockSpec(shape, lambda *_: (0,) * nd, pipeline_mode=pl.Buffered(1))


def _tile(m, pref):
    t = min(m, pref)
    while m % t or t % SUBLANES:
        t -= 1
    return t


def _dot(a, b):
    return jnp.dot(a, b, preferred_element_type=F32)


def _dot_nt(a, b):
    return lax.dot_general(a, b, (((1,), (1,)), ((), ())), preferred_element_type=F32)


def _rms(x, g):
    return x * lax.rsqrt(jnp.mean(x * x, axis=-1, keepdims=True) + RMS_EPS) * g


def _mla_in_kernel(x_ref, g_ref, w_ref, qg_ref, kvg_ref, tab_ref, rows_ref, cq_ref, *key_refs, q_lora, kv_lora, rope):
    h = _rms(x_ref[...], g_ref[...]).astype(BF16)
    z = _dot(h, w_ref[...])
    cq = _rms(z[:, :q_lora], qg_ref[...])
    ckv = _rms(z[:, q_lora:q_lora + kv_lora], kvg_ref[...])
    t = z[:, q_lora + kv_lora:] * tab_ref[...]
    kpe2 = t + pltpu.roll(t, rope, axis=1)
    cq_ref[...] = cq.astype(BF16)
    rows_ref[:, :kv_lora] = ckv
    rows_ref[:, kv_lora:] = kpe2[:, :rope]
    if key_refs:
        ckv_ref, kt_ref = key_refs
        ckv_ref[...] = ckv.astype(BF16)
        kt_ref[:kv_lora, :] = ckv.T.astype(BF16)
        kt_ref[kv_lora:, :] = kpe2.T[:rope].astype(BF16)


def _mla_in(x, g, w_in_ext, q_g, kv_g, tab, *, q_lora, kv_lora, rope, with_keys):
    m, d = x.shape
    tm = _tile(m, 512)
    n = w_in_ext.shape[1]
    row = lambda i: (i, 0)
    out_specs = [pl.BlockSpec((tm, kv_lora + rope), row), pl.BlockSpec((tm, q_lora), row)]
    out_shape = [jax.ShapeDtypeStruct((m, kv_lora + rope), F32), jax.ShapeDtypeStruct((m, q_lora), BF16)]
    if with_keys:
        out_specs += [pl.BlockSpec((tm, kv_lora), row), pl.BlockSpec((None, kv_lora + rope, tm), lambda i: (i, 0, 0))]
        out_shape += [jax.ShapeDtypeStruct((m, kv_lora), BF16), jax.ShapeDtypeStruct((m // tm, kv_lora + rope, tm), BF16)]
    return pl.pallas_call(
        functools.partial(_mla_in_kernel, q_lora=q_lora, kv_lora=kv_lora, rope=rope),
        grid=(m // tm,),
        in_specs=[
            pl.BlockSpec((tm, d), row),
            _const_spec((1, d)),
            _const_spec((d, n)),
            _const_spec((1, q_lora)),
            _const_spec((1, kv_lora)),
            pl.BlockSpec((tm, 2 * rope), row),
        ],
        out_specs=out_specs,
        out_shape=out_shape,
        compiler_params=_params(1),
        name="mla_in",
    )(x, g, w_in_ext, q_g, kv_g, tab)


def _mla_q_kernel(cq_ref, wuq_ref, wuk_ref, tab_ref, qlat_ref, qpe_ref, *, heads, nope, rope, scale):
    q = _dot(cq_ref[...], wuq_ref[...])
    tab = tab_ref[...]
    for h in range(heads):
        qn = q[:, h * nope:(h + 1) * nope].astype(BF16)
        qlat_ref[h] = (_dot(qn, wuk_ref[h]) * scale).astype(qlat_ref.dtype)
        lo = heads * nope + h * 2 * rope
        t = q[:, lo:lo + 2 * rope] * tab
        qpe_ref[h] = ((t + pltpu.roll(t, rope, axis=1))[:, :rope] * scale).astype(qpe_ref.dtype)


def _mla_q(cq, w_uq_ext, w_uk_t, tab, *, heads, nope, rope, scale, out_dtype):
    m, q_lora = cq.shape
    kv_lora = w_uk_t.shape[2]
    tm = _tile(m, 256)
    return pl.pallas_call(
        functools.partial(_mla_q_kernel, heads=heads, nope=nope, rope=rope, scale=scale),
        grid=(m // tm,),
        in_specs=[
            pl.BlockSpec((tm, q_lora), lambda i: (i, 0)),
            _const_spec(w_uq_ext.shape),
            _const_spec(w_uk_t.shape),
            pl.BlockSpec((tm, 2 * rope), lambda i: (i, 0)),
        ],
        out_specs=[
            pl.BlockSpec((heads, tm, kv_lora), lambda i: (0, i, 0)),
            pl.BlockSpec((heads, tm, rope), lambda i: (0, i, 0)),
        ],
        out_shape=[
            jax.ShapeDtypeStruct((heads, m, kv_lora), out_dtype),
            jax.ShapeDtypeStruct((heads, m, rope), out_dtype),
        ],
        compiler_params=_params(1),
        name="mla_q",
    )(cq, w_uq_ext, w_uk_t, tab)


def _mla_prompt_attn_kernel(qlat_ref, qpe_ref, kt_ref, ckv_ref, o_ref, m_sc, l_sc, acc_sc, *, tq, tk, heads, group):
    i = pl.program_id(1)
    j = pl.program_id(2)
    nk = pl.num_programs(2)
    kv_lora = ckv_ref.shape[1]
    rope = qpe_ref.shape[2]
    last_tile = ((i + 1) * tq - 1) // tk

    @pl.when(j == 0)
    def _():
        m_sc[...] = jnp.full(m_sc.shape, NEG_INF, F32)
        l_sc[...] = jnp.zeros(l_sc.shape, F32)
        acc_sc[...] = jnp.zeros(acc_sc.shape, F32)

    @pl.when(j <= last_tile)
    def _():
        qpos = i * tq + lax.broadcasted_iota(jnp.int32, (tq, tk), 0)
        kpos = j * tk + lax.broadcasted_iota(jnp.int32, (tq, tk), 1)
        bias = jnp.where(kpos <= qpos, 0.0, NEG_INF)
        ckv = ckv_ref[...]
        kt_c = kt_ref[:kv_lora, :]
        kt_p = kt_ref[kv_lora:, :]
        def logits(g):
            hs = slice(g * group, (g + 1) * group)
            q = qlat_ref[hs].reshape(group * tq, kv_lora)
            qp = qpe_ref[hs].reshape(group * tq, rope)
            return _dot(q, kt_c) + _dot(qp, kt_p)

        n_groups = heads // group
        s_next = logits(0)
        for g in range(n_groups):
            s = s_next
            if g + 1 < n_groups:
                s_next = logits(g + 1)
            s = (s.reshape(group, tq, tk) + bias[None]).reshape(group * tq, tk)
            m_prev = m_sc[g]
            m_new = jnp.maximum(m_prev, s.max(axis=1, keepdims=True))
            alpha = jnp.exp2(m_prev - m_new)
            p = jnp.exp2(s - m_new)
            l_sc[g] = alpha * l_sc[g] + p.sum(axis=1, keepdims=True)
            acc_sc[g] = alpha * acc_sc[g] + _dot(p.astype(BF16), ckv)
            m_sc[g] = m_new

    @pl.when(j == nk - 1)
    def _():
        for g in range(heads // group):
            o = acc_sc[g] / l_sc[g]
            o_ref[g * group:(g + 1) * group] = o.reshape(group, tq, kv_lora).astype(o_ref.dtype)


def _mla_prompt_attn(qlat, qpe, kt, ckv, *, batch, seq):
    heads, m, kv_lora = qlat.shape
    rope = qpe.shape[2]
    tq = _tile(seq, 256)
    tk = kt.shape[2]
    assert seq % tk == 0 and tk % tq == 0
    nq, nk = seq // tq, seq // tk
    group = max(1, 512 // tq)
    while heads % group:
        group -= 1

    def key_tile(b, i, j):
        return b * nk + jnp.minimum(j, ((i + 1) * tq - 1) // tk)

    qmap = lambda b, i, j: (0, b * nq + i, 0)
    return pl.pallas_call(
        functools.partial(_mla_prompt_attn_kernel, tq=tq, tk=tk, heads=heads, group=group),
        grid=(batch, nq, nk),
        in_specs=[
            pl.BlockSpec((heads, tq, kv_lora), qmap),
            pl.BlockSpec((heads, tq, rope), qmap),
            pl.BlockSpec((None, kv_lora + rope, tk), lambda b, i, j: (key_tile(b, i, j), 0, 0)),
            pl.BlockSpec((tk, kv_lora), lambda b, i, j: (key_tile(b, i, j), 0)),
        ],
        out_specs=pl.BlockSpec((heads, tq, kv_lora), qmap),
        out_shape=jax.ShapeDtypeStruct((heads, m, kv_lora), BF16),
        scratch_shapes=[
            pltpu.VMEM((heads // group, group * tq, 1), F32),
            pltpu.VMEM((heads // group, group * tq, 1), F32),
            pltpu.VMEM((heads // group, group * tq, kv_lora), F32),
        ],
        compiler_params=_params(3),
        name="mla_prompt_attn",
    )(qlat, qpe, kt, ckv)


def _mla_sample_attn_kernel(pt_ref, qlat_ref, qpe_ref, new_ref, *rest, pg, heads, tq, page):
    del pt_ref
    page_refs = rest[:pg]
    o_ref, kt_sc, m_sc, l_sc, acc_sc = rest[pg:]
    j = pl.program_id(1)
    nj = pl.num_programs(1)
    kv_lora = qlat_ref.shape[2]
    rope = qpe_ref.shape[2]
    rows = heads * tq

    @pl.when(j == 0)
    def _():
        m_sc[...] = jnp.full(m_sc.shape, NEG_INF, F32)
        l_sc[...] = jnp.zeros(l_sc.shape, F32)
        acc_sc[...] = jnp.zeros(acc_sc.shape, F32)

    q = qlat_ref[...].reshape(rows, kv_lora).astype(BF16)
    qp = qpe_ref[...].reshape(rows, rope).astype(BF16)

    def update(s, pv):
        m_prev = m_sc[...]
        m_new = jnp.maximum(m_prev, s.max(axis=1, keepdims=True))
        alpha = jnp.exp2(m_prev - m_new)
        p = jnp.exp2(s - m_new)
        l_sc[...] = alpha * l_sc[...] + p.sum(axis=1, keepdims=True)
        acc_sc[...] = alpha * acc_sc[...] + pv(p.astype(BF16))
        m_sc[...] = m_new

    n_slabs = 2 if pg % 2 == 0 else 1
    per = pg // n_slabs
    logits, values = [], []
    for h in range(n_slabs):
        for k in range(h * per, (h + 1) * per):
            kt_sc[:, k * page:(k + 1) * page] = page_refs[k][...].astype(BF16)
        cols = slice(h * per * page, (h + 1) * per * page)
        ckv_t = kt_sc[:kv_lora, cols]
        logits.append(_dot(q, ckv_t) + _dot(qp, kt_sc[kv_lora:, cols]))
        values.append(ckv_t)
    for s, ckv_t in zip(logits, values):
        update(s, lambda p, v=ckv_t: _dot_nt(p, v))

    @pl.when(j == nj - 1)
    def _():
        k_new = jnp.concatenate([new_ref[...], jnp.zeros((page - tq, kv_lora + rope), F32)], axis=0)
        ckv = k_new[:, :kv_lora].astype(BF16)
        s = _dot_nt(q, ckv) + _dot_nt(qp, k_new[:, kv_lora:].astype(BF16))
        t_q = lax.broadcasted_iota(jnp.int32, (tq, page), 0)
        t_k = lax.broadcasted_iota(jnp.int32, (tq, page), 1)
        s = jnp.where((t_k <= t_q)[None], s.reshape(heads, tq, page), NEG_INF).reshape(rows, page)
        update(s, lambda p: _dot(p, ckv))
        o = acc_sc[...] / l_sc[...]
        o_ref[...] = o.reshape(heads, tq, kv_lora).astype(o_ref.dtype)


def _mla_sample_attn(qlat, qpe, rows_new, cache_t, page_table, *, layer, tq, pages_per_step):
    heads, m, kv_lora = qlat.shape
    rope = qpe.shape[2]
    n_seq, n_pages = page_table.shape
    page = cache_t.shape[3]
    pg = pages_per_step
    while n_pages % pg:
        pg -= 1
    qmap = lambda s, j, pt: (0, s, 0)

    def page_spec(k):
        return pl.BlockSpec((None, None, kv_lora + rope, page), lambda s, j, pt: (layer, pt[s * n_pages + j * pg + k], 0, 0))

    grid_spec = pltpu.PrefetchScalarGridSpec(
        num_scalar_prefetch=1,
        grid=(n_seq, n_pages // pg),
        in_specs=[
            pl.BlockSpec((heads, tq, kv_lora), qmap),
            pl.BlockSpec((heads, tq, rope), qmap),
            pl.BlockSpec((tq, kv_lora + rope), lambda s, j, pt: (s, 0)),
        ] + [page_spec(k) for k in range(pg)],
        out_specs=pl.BlockSpec((heads, tq, kv_lora), qmap),
        scratch_shapes=[
            pltpu.VMEM((kv_lora + rope, pg * page), BF16),
            pltpu.VMEM((heads * tq, 1), F32),
            pltpu.VMEM((heads * tq, 1), F32),
            pltpu.VMEM((heads * tq, kv_lora), F32),
        ],
    )
    return pl.pallas_call(
        functools.partial(_mla_sample_attn_kernel, pg=pg, heads=heads, tq=tq, page=page),
        grid_spec=grid_spec,
        out_shape=jax.ShapeDtypeStruct((heads, m, kv_lora), F32),
        compiler_params=_params(2),
        name="mla_sample_attn",
    )(page_table.reshape(-1), qlat, qpe, rows_new, *([cache_t] * pg))


def _mla_out_kernel(o_ref, wuv_ref, wo_ref, x_ref, g_ref, out_ref, *, heads):
    v = jnp.concatenate([_dot(o_ref[h].astype(BF16), wuv_ref[h]).astype(BF16) for h in range(heads)], axis=1)
    out_ref[...] = x_ref[...] + _rms(_dot(v, wo_ref[...]), g_ref[...])


def _mla_out(o_lat, w_uv_t, w_o, x, g):
    heads, m, kv_lora = o_lat.shape
    d = x.shape[1]
    tm = _tile(m, 256)
    return pl.pallas_call(
        functools.partial(_mla_out_kernel, heads=heads),
        grid=(m // tm,),
        in_specs=[
            pl.BlockSpec((heads, tm, kv_lora), lambda i: (0, i, 0)),
            _const_spec(w_uv_t.shape),
            _const_spec(w_o.shape),
            pl.BlockSpec((tm, d), lambda i: (i, 0)),
            _const_spec((1, d)),
        ],
        out_specs=pl.BlockSpec((tm, d), lambda i: (i, 0)),
        out_shape=jax.ShapeDtypeStruct((m, d), F32),
        compiler_params=_params(1),
        name="mla_out",
    )(o_lat, w_uv_t, w_o, x, g)


def _out_proj_kernel(a_ref, w_ref, b_ref, x_ref, g_ref, out_ref):
    y = _dot(a_ref[...].astype(BF16), w_ref[...]) + b_ref[...]
    out_ref[...] = x_ref[...] + _rms(y, g_ref[...])


def _out_proj(a, w, b, x, g):
    m, k = a.shape
    d = x.shape[1]
    tm = _tile(m, 512)
    return pl.pallas_call(
        _out_proj_kernel,
        grid=(m // tm,),
        in_specs=[
            pl.BlockSpec((tm, k), lambda i: (i, 0)),
            _const_spec(w.shape),
            _const_spec((1, d)),
            pl.BlockSpec((tm, d), lambda i: (i, 0)),
            _const_spec((1, d)),
        ],
        out_specs=pl.BlockSpec((tm, d), lambda i: (i, 0)),
        out_shape=jax.ShapeDtypeStruct((m, d), F32),
        compiler_params=_params(1),
        name="out_proj",
    )(a, w, b, x, g)


def _norm_matmul_kernel(*refs, has_bias, has_rope, rot_half, scale):
    x_ref, g_ref, w_ref = refs[:3]
    refs = refs[3:]
    if has_bias:
        b_ref, refs = refs[0], refs[1:]
    if has_rope:
        (c_ref, s1_ref, s2_ref), refs = refs[:3], refs[3:]
    out_ref, h_sc = refs

    @pl.when(pl.program_id(1) == 0)
    def _():
        h_sc[...] = _rms(x_ref[...], g_ref[...]).astype(BF16)

    y = _dot(h_sc[...], w_ref[...])
    if has_bias:
        y = y + b_ref[...]
    if has_rope:
        c, s1, s2 = c_ref[...], s1_ref[...], s2_ref[...]
        for k in range(y.shape[1] // LANES):
            yc = y[:, k * LANES:(k + 1) * LANES]
            yc = yc * c + pltpu.roll(yc, LANES - rot_half, axis=1) * s1 + pltpu.roll(yc, rot_half, axis=1) * s2
            out_ref[:, k * LANES:(k + 1) * LANES] = (yc * scale).astype(out_ref.dtype)
    else:
        out_ref[...] = (y * scale).astype(out_ref.dtype) if scale != 1.0 else y.astype(out_ref.dtype)


def _norm_matmul(x, g, w, b=None, rope_tabs=None, *, rot_half=0, scale=1.0, out_dtype=F32, tm_pref=1024, tn_pref=1024):
    m, d = x.shape
    n = w.shape[1]
    tm = _tile(m, tm_pref)
    tn = min(n, tn_pref)
    while n % tn or tn % LANES:
        tn -= LANES
    in_specs = [
        pl.BlockSpec((tm, d), lambda i, j: (i, 0)),
        _const_spec((1, d)),
        pl.BlockSpec((d, tn), lambda i, j: (0, j)),
    ]
    args = [x, g, w]
    if b is not None:
        in_specs.append(pl.BlockSpec((1, tn), lambda i, j: (0, j)))
        args.append(b)
    if rope_tabs is not None:
        in_specs += [pl.BlockSpec((tm, LANES), lambda i, j: (i, 0))] * 3
        args += list(rope_tabs)
    return pl.pallas_call(
        functools.partial(_norm_matmul_kernel, has_bias=b is not None, has_rope=rope_tabs is not None, rot_half=rot_half, scale=scale),
        grid=(m // tm, n // tn),
        in_specs=in_specs,
        out_specs=pl.BlockSpec((tm, tn), lambda i, j: (i, j)),
        out_shape=jax.ShapeDtypeStruct((m, n), out_dtype),
        scratch_shapes=[pltpu.VMEM((tm, d), BF16)],
        compiler_params=_params(2),
        name="norm_matmul",
    )(*args)


def _ffn_kernel(*refs, per_seq_rows, tiles_per_seq):
    if tiles_per_seq:
        x_ref, gpre_ref, wg_ref, wu_ref, cw_ref, cb_ref, wd_ref, gpost_ref, out_ref, tail_ref, h_sc, acc_sc, carry_sc = refs
    else:
        x_ref, gpre_ref, wg_ref, wu_ref, prev_ref, cw_ref, cb_ref, wd_ref, gpost_ref, out_ref, tail_ref, h_sc, acc_sc = refs
    i = pl.program_id(0)
    f = pl.program_id(1)
    nf = pl.num_programs(1)

    @pl.when(f == 0)
    def _():
        h_sc[...] = _rms(x_ref[...], gpre_ref[...]).astype(BF16)
        acc_sc[...] = jnp.zeros(acc_sc.shape, F32)

    tm = h_sc.shape[0]
    unit = SUBLANES if tiles_per_seq else per_seq_rows
    half = FFN_ROW_BLOCK if tm % FFN_ROW_BLOCK == 0 and FFN_ROW_BLOCK % unit == 0 else tm
    blocks = [slice(k * half, (k + 1) * half) for k in range(tm // half)]
    gates, ups = [], []
    for rows in blocks:
        hb = h_sc[rows, :]
        gates.append(_dot(hb, wg_ref[...]))
        ups.append(_dot(hb, wu_ref[...]))
    last8 = gates[-1][half - SUBLANES:, :]
    if tiles_per_seq:
        tail_ref[...] = last8

        @pl.when(i % tiles_per_seq == 0)
        def _():
            carry_sc[f] = jnp.zeros(last8.shape, F32)

        before = carry_sc[f]
        carry_sc[f] = last8
    for k, rows in enumerate(blocks):
        gate, up = gates[k], ups[k]
        r = lax.broadcasted_iota(jnp.int32, gate.shape, 0)
        d1 = pltpu.roll(gate, 1, axis=0)
        d2 = pltpu.roll(gate, 2, axis=0)
        if tiles_per_seq:
            h6 = before[SUBLANES - 2:SUBLANES - 1, :]
            h7 = before[SUBLANES - 1:SUBLANES, :]
            g1 = jnp.where(r == 0, h7, d1)
            g2 = jnp.where(r == 0, h6, jnp.where(r == 1, h7, d2))
            before = gate[half - SUBLANES:, :]
        else:
            tail_ref[rows, :] = gate
            t = r % per_seq_rows
            p2 = prev_ref[rows, :]
            g1 = jnp.where(t == 0, pltpu.roll(p2, half - 1, axis=0), d1)
            g2 = jnp.where(t < 2, p2, d2)
        conv = cb_ref[...] + g2 * cw_ref[0:1, :] + g1 * cw_ref[1:2, :] + gate * cw_ref[2:3, :]
        act = conv * (1.0 / (1.0 + jnp.exp(-conv))) * up
        acc_sc[rows, :] += _dot(act.astype(BF16), wd_ref[...])

    @pl.when(f == nf - 1)
    def _():
        out_ref[...] = x_ref[...] + _rms(acc_sc[...], gpost_ref[...])


def _ffn(x, g_pre, w_up, prev, conv_w, conv_b, w_down, g_post, *, layer, seq_rows):
    m, d = x.shape
    d_ff = w_down.shape[1]
    tf = _tile(d_ff, 512)
    nf = d_ff // tf
    tm = _tile(seq_rows, FFN_ROW_TILE) if prev is None else _tile(m, FFN_ROW_TILE // 2)
    in_specs = [
        pl.BlockSpec((tm, d), lambda i, f: (i, 0), pipeline_mode=pl.Buffered(1)),
        _const_spec((1, d)),
        pl.BlockSpec((None, d, tf), lambda i, f: (layer, 0, f)),
        pl.BlockSpec((None, d, tf), lambda i, f: (layer, 0, nf + f)),
    ]
    args = [x, g_pre, w_up, w_up]
    scratch = [pltpu.VMEM((tm, d), BF16), pltpu.VMEM((tm, d), F32)]
    if prev is None:
        tiles_per_seq = seq_rows // tm
        scratch.append(pltpu.VMEM((nf, SUBLANES, tf), F32))
        tail_spec = pl.BlockSpec((None, SUBLANES, tf), lambda i, f: (i, 0, f))
        tail_shape = jax.ShapeDtypeStruct((m // tm, SUBLANES, d_ff), F32)
    else:
        assert tm % seq_rows == 0 and seq_rows >= 2
        tiles_per_seq = 0
        in_specs.append(pl.BlockSpec((tm, tf), lambda i, f: (i, f)))
        args.append(prev)
        tail_spec = pl.BlockSpec((tm, tf), lambda i, f: (i, f))
        tail_shape = jax.ShapeDtypeStruct((m, d_ff), F32)
    in_specs += [
        pl.BlockSpec((None, conv_w.shape[1], tf), lambda i, f: (layer, 0, f)),
        pl.BlockSpec((None, 1, tf), lambda i, f: (layer, 0, f)),
        pl.BlockSpec((None, tf, d), lambda i, f: (layer, f, 0)),
        _const_spec((1, d)),
    ]
    args += [conv_w, conv_b, w_down, g_post]
    return pl.pallas_call(
        functools.partial(_ffn_kernel, per_seq_rows=seq_rows, tiles_per_seq=tiles_per_seq),
        grid=(m // tm, nf),
        in_specs=in_specs,
        out_specs=[pl.BlockSpec((tm, d), lambda i, f: (i, 0), pipeline_mode=pl.Buffered(1)), tail_spec],
        out_shape=[jax.ShapeDtypeStruct((m, d), F32), tail_shape],
        scratch_shapes=scratch,
        compiler_params=_params(2),
        name="ffn",
    )(*args)


def _sink_softmax(s_list, sink):
    m = sink
    for s in s_list:
        m = jnp.maximum(m, s.max(axis=1, keepdims=True))
    denom = jnp.exp(sink - m)
    es = []
    for s in s_list:
        e = jnp.exp(s - m)
        es.append(e)
        denom = denom + e.sum(axis=1, keepdims=True)
    return [(e / denom).astype(BF16) for e in es]


def _swa_prompt_kernel(sink_ref, q_ref, kc_ref, kp_ref, vc_ref, vp_ref, o_ref, *, kv_heads, group, hd):
    i = pl.program_id(1)
    w = q_ref.shape[0]
    qi = lax.broadcasted_iota(jnp.int32, (w, 2 * w), 0) + w
    ki = lax.broadcasted_iota(jnp.int32, (w, 2 * w), 1)
    dist = qi - ki
    mask = (dist >= 0) & (dist < w) & ((ki >= w) | (i > 0))
    def scores(kv):
        sl = slice(kv * hd, (kv + 1) * hd)
        kk = jnp.concatenate([kp_ref[:, sl], kc_ref[:, sl]], axis=0).astype(BF16)
        return [_dot_nt(q_ref[:, (kv * group + g) * hd:(kv * group + g + 1) * hd], kk) for g in range(group)]

    s_next = scores(0)
    for kv in range(kv_heads):
        s_cur = s_next
        if kv + 1 < kv_heads:
            s_next = scores(kv + 1)
        sl = slice(kv * hd, (kv + 1) * hd)
        vv = jnp.concatenate([vp_ref[:, sl], vc_ref[:, sl]], axis=0).astype(BF16)
        outs = []
        for g in range(group):
            p = _sink_softmax([jnp.where(mask, s_cur[g], NEG_INF)], sink_ref[kv * group + g])[0]
            outs.append(_dot(p, vv))
        o_ref[:, kv * group * hd:(kv + 1) * group * hd] = jnp.concatenate(outs, axis=1).astype(o_ref.dtype)


def _swa_prompt_attn(q, k, v, sinks, *, batch, seq, window, kv_heads, hd):
    m, dq = q.shape
    dk = k.shape[1]
    nb = seq // window
    group = dq // (kv_heads * hd)
    cur = lambda b, i, sk: (b * nb + i, 0)
    prev = lambda b, i, sk: (jnp.maximum(b * nb + i - 1, 0), 0)
    grid_spec = pltpu.PrefetchScalarGridSpec(
        num_scalar_prefetch=1,
        grid=(batch, nb),
        in_specs=[
            pl.BlockSpec((window, dq), cur),
            pl.BlockSpec((window, dk), cur),
            pl.BlockSpec((window, dk), prev),
            pl.BlockSpec((window, dk), cur),
            pl.BlockSpec((window, dk), prev),
        ],
        out_specs=pl.BlockSpec((window, dq), cur),
    )
    return pl.pallas_call(
        functools.partial(_swa_prompt_kernel, kv_heads=kv_heads, group=group, hd=hd),
        grid_spec=grid_spec,
        out_shape=jax.ShapeDtypeStruct((m, dq), BF16),
        compiler_params=_params(2),
        name="swa_prompt_attn",
    )(sinks.reshape(-1), q, k, k, v, v)


def _swa_sample_kernel(sink_ref, q_ref, st_ref, kn_ref, vn_ref, o_ref, *, tq, kv_heads, group, hd):
    n_seq, w, _ = st_ref.shape
    dk = kv_heads * hd
    rows = group * tq
    t_q = lax.broadcasted_iota(jnp.int32, (rows, w), 0) % tq
    col = lax.broadcasted_iota(jnp.int32, (rows, w), 1)
    grp = lax.broadcasted_iota(jnp.int32, (rows, 1), 0) // tq
    pad = jnp.zeros((w - tq, hd), F32)
    sinks = []
    for kv in range(kv_heads):
        sink = jnp.zeros((rows, 1), F32)
        for g in range(group):
            sink = jnp.where(grp == g, sink_ref[kv * group + g], sink)
        sinks.append(sink)
    scores = []
    for b in range(n_seq):
        rs = slice(b * tq, (b + 1) * tq)
        for kv in range(kv_heads):
            sl = slice(kv * hd, (kv + 1) * hd)
            k_old = st_ref[b, :, sl].astype(BF16)
            k_new = jnp.concatenate([kn_ref[rs, sl], pad], axis=0).astype(BF16)
            qs = jnp.concatenate([q_ref[rs, (kv * group + g) * hd:(kv * group + g + 1) * hd] for g in range(group)], axis=0).astype(BF16)
            scores.append([jnp.where(col > t_q, _dot_nt(qs, k_old), NEG_INF), jnp.where(col <= t_q, _dot_nt(qs, k_new), NEG_INF)])
    probs = [_sink_softmax(s, sinks[n % kv_heads]) for n, s in enumerate(scores)]
    for b in range(n_seq):
        rs = slice(b * tq, (b + 1) * tq)
        outs = []
        for kv in range(kv_heads):
            sl = slice(kv * hd, (kv + 1) * hd)
            v_old = st_ref[b, :, dk + kv * hd:dk + (kv + 1) * hd].astype(BF16)
            v_new = jnp.concatenate([vn_ref[rs, sl], pad], axis=0).astype(BF16)
            p_old, p_new = probs[b * kv_heads + kv]
            o = _dot(p_old, v_old) + _dot(p_new, v_new)
            outs += [o[g * tq:(g + 1) * tq] for g in range(group)]
        o_ref[rs, :] = jnp.concatenate(outs, axis=1).astype(o_ref.dtype)


def _swa_sample_attn(q, state, k_new, v_new, sinks, *, tq, kv_heads, hd):
    m, dq = q.shape
    n_seq, window, dkv = state.shape
    dk = k_new.shape[1]
    group = dq // (kv_heads * hd)
    nb = 4
    while n_seq % nb:
        nb -= 1
    row = lambda s, sk: (s, 0)
    grid_spec = pltpu.PrefetchScalarGridSpec(
        num_scalar_prefetch=1,
        grid=(n_seq // nb,),
        in_specs=[
            pl.BlockSpec((nb * tq, dq), row),
            pl.BlockSpec((nb, window, dkv), lambda s, sk: (s, 0, 0)),
            pl.BlockSpec((nb * tq, dk), row),
            pl.BlockSpec((nb * tq, dk), row),
        ],
        out_specs=pl.BlockSpec((nb * tq, dq), row),
    )
    return pl.pallas_call(
        functools.partial(_swa_sample_kernel, tq=tq, kv_heads=kv_heads, group=group, hd=hd),
        grid_spec=grid_spec,
        out_shape=jax.ShapeDtypeStruct((m, dq), F32),
        compiler_params=_params(1),
        name="swa_sample_attn",
    )(sinks.reshape(-1), q, state, k_new, v_new)


def _rope_tables(pos, dim, theta):
    inv = theta ** (-jnp.arange(0, dim, 2, dtype=F32) / dim)
    ang = pos.astype(F32)[:, None] * inv[None, :]
    return jnp.cos(ang), jnp.sin(ang)


def _rot_cols(w, half):
    return jnp.concatenate([-w[..., half:], w[..., :half]], axis=-1)


def kernel(x_prompt, x_sample, cache_mla, state_swa_kv, state_ffn, page_table, ln_mix_pre, ln_mix_post, ln_ffn_pre, ln_ffn_post, w_ffn_up, ffn_conv_w, ffn_conv_b, w_ffn_down, w_mla_in, mla_q_norm, mla_kv_norm, w_mla_uq, w_mla_uk, w_mla_uv, w_mla_o, swa_kv_norm, w_swa_kv, b_swa_kv, w_swa_q, b_swa_q, swa_sinks, w_swa_o, b_swa_o):
    batch, seq, d = x_prompt.shape
    n_seq, tq_s, _ = x_sample.shape
    depth = ln_mix_pre.shape[0]
    n_a = w_mla_in.shape[0]
    kv_lora, heads, nope = w_mla_uk.shape[1:]
    v_dim = w_mla_uv.shape[3]
    q_lora = mla_q_norm.shape[1]
    rope = w_mla_in.shape[2] - q_lora - kv_lora
    page = cache_mla.shape[2]
    past_len = page_table.shape[1] * page
    window, _, kv_heads, hd = state_swa_kv.shape[1:]
    dk = kv_heads * hd
    rot_half = hd // 8
    d_ff = w_ffn_down.shape[1]
    mla_scale = float((nope + rope) ** -0.5)
    swa_scale = float(hd ** -0.5)
    assert 2 * rope == LANES and 2 * hd == LANES and tq_s == SUBLANES and seq % window == 0

    xs = {"p": x_prompt.reshape(batch * seq, d), "s": x_sample.reshape(n_seq * tq_s, d)}
    pos = {"p": jnp.tile(jnp.arange(seq), batch), "s": jnp.tile(past_len + jnp.arange(tq_s), n_seq)}
    mla_tab, swa_tabs = {}, {}
    for st in ("p", "s"):
        cos, sin = _rope_tables(pos[st], rope, MLA_THETA)
        mla_tab[st] = jnp.concatenate([cos, cos, sin, sin], axis=1)
        cos, sin = _rope_tables(pos[st], 2 * rot_half, ROPE_THETA)
        one = jnp.ones((cos.shape[0], hd - 2 * rot_half), F32)
        zero = jnp.zeros_like(one)
        z8 = jnp.zeros_like(sin)
        swa_tabs[st] = tuple(
            jnp.tile(jnp.concatenate(parts, axis=1), (1, LANES // hd))
            for parts in ([cos, cos, one], [-sin, z8, zero], [z8, sin, zero])
        )
    row2 = lambda a: a.reshape(1, -1)

    mla_rows = {"p": [], "s": []}
    ffn_state = {"p": [], "s": []}
    k_sh, v_sh = {}, {}
    swa_state = state_swa_kv.reshape(n_seq, window, 2 * dk)
    cache_t = jnp.swapaxes(cache_mla, 2, 3)
    w_up_bf = w_ffn_up.astype(BF16)
    w_down_bf = w_ffn_down.astype(BF16)
    conv_b3 = ffn_conv_b.reshape(depth, 1, d_ff)
    for l in range(depth):
        if l < n_a:
            a = l
            w_in = w_mla_in[a]
            w_pe = w_in[:, q_lora + kv_lora:]
            w_in_ext = jnp.concatenate([w_in, _rot_cols(w_pe, rope // 2)], axis=1).astype(BF16)
            w_uq = w_mla_uq[a].reshape(q_lora, heads, nope + rope)
            w_uq_pe = w_uq[..., nope:]
            w_uq_ext = jnp.concatenate(
                [w_uq[..., :nope].reshape(q_lora, heads * nope),
                 jnp.concatenate([w_uq_pe, _rot_cols(w_uq_pe, rope // 2)], axis=-1).reshape(q_lora, heads * 2 * rope)],
                axis=1).astype(BF16)
            w_uk_t = jnp.transpose(w_mla_uk[a], (1, 2, 0)).astype(BF16)
            w_uv_t = jnp.transpose(w_mla_uv[a], (1, 0, 2)).astype(BF16)
            w_o = w_mla_o[a].astype(BF16)
            for st in ("p", "s"):
                rows, cq, *keys = _mla_in(xs[st], row2(ln_mix_pre[l]), w_in_ext, row2(mla_q_norm[a]), row2(mla_kv_norm[a]),
                                          mla_tab[st], q_lora=q_lora, kv_lora=kv_lora, rope=rope, with_keys=st == "p")
                mla_rows[st].append(rows)
                qlat, qpe = _mla_q(cq, w_uq_ext, w_uk_t, mla_tab[st], heads=heads, nope=nope, rope=rope, scale=mla_scale * LOG2E,
                                   out_dtype=BF16 if st == "p" else F32)
                if st == "p":
                    ckv, kt = keys
                    o_lat = _mla_prompt_attn(qlat, qpe, kt, ckv, batch=batch, seq=seq)
                else:
                    o_lat = _mla_sample_attn(qlat, qpe, rows, cache_t, page_table, layer=a, tq=tq_s, pages_per_step=MLA_PAGES_PER_STEP)
                xs[st] = _mla_out(o_lat, w_uv_t, w_o, xs[st], row2(ln_mix_post[l]))
        else:
            b = l - n_a
            if l == n_a:
                w_k = w_swa_kv[:, :dk].astype(BF16)
                w_v = w_swa_kv[:, dk:].astype(BF16)
                for st in ("p", "s"):
                    k_sh[st] = _norm_matmul(xs[st], row2(swa_kv_norm), w_k, row2(b_swa_kv[:dk]), swa_tabs[st], rot_half=rot_half)
                    v_sh[st] = _norm_matmul(xs[st], row2(swa_kv_norm), w_v, row2(b_swa_kv[dk:]))
            w_q = w_swa_q[b].astype(BF16)
            w_o = w_swa_o[b].astype(BF16)
            for st in ("p", "s"):
                q = _norm_matmul(xs[st], row2(ln_mix_pre[l]), w_q, row2(b_swa_q[b]), swa_tabs[st], rot_half=rot_half,
                                 scale=swa_scale, out_dtype=BF16 if st == "p" else F32)
                if st == "p":
                    attn = _swa_prompt_attn(q, k_sh[st], v_sh[st], swa_sinks[b], batch=batch, seq=seq, window=window,
                                            kv_heads=kv_heads, hd=hd)
                else:
                    attn = _swa_sample_attn(q, swa_state, k_sh[st], v_sh[st], swa_sinks[b], tq=tq_s, kv_heads=kv_heads, hd=hd)
                xs[st] = _out_proj(attn, w_o, row2(b_swa_o[b]), xs[st], row2(ln_mix_post[l]))
        for st in ("p", "s"):
            if st == "p":
                prev = None
            else:
                prev = jnp.pad(state_ffn[l], ((0, 0), (0, tq_s - state_ffn.shape[2]), (0, 0))).reshape(n_seq * tq_s, d_ff)
            xs[st], tail = _ffn(xs[st], row2(ln_ffn_pre[l]), w_up_bf, prev, ffn_conv_w, conv_b3, w_down_bf,
                                row2(ln_ffn_post[l]), layer=l, seq_rows=seq if st == "p" else tq_s)
            if st == "p":
                ffn_state[st].append(tail.reshape(batch, -1, SUBLANES, d_ff)[:, -1, SUBLANES - 2:])
            else:
                ffn_state[st].append(tail.reshape(n_seq, tq_s, d_ff)[:, tq_s - 2:])

    kv_p = jnp.stack([a.reshape(batch, seq, kv_heads, hd)[:, seq - window:] for a in (k_sh["p"], v_sh["p"])], axis=2)
    kv_s_new = jnp.stack([k_sh["s"].reshape(n_seq, tq_s, kv_heads, hd), v_sh["s"].reshape(n_seq, tq_s, kv_heads, hd)], axis=2)
    return (
        xs["p"].reshape(batch, seq, d),
        xs["s"].reshape(n_seq, tq_s, d),
        jnp.stack(mla_rows["p"], axis=0).reshape(n_a, batch, seq, kv_lora + rope),
        jnp.stack(mla_rows["s"], axis=0).reshape(n_a, n_seq, tq_s, kv_lora + rope),
        kv_p,
        jnp.concatenate([state_swa_kv, kv_s_new], axis=1)[:, tq_s:],
        jnp.stack(ffn_state["p"], axis=0),
        jnp.stack(ffn_state["s"], axis=0),
    )
```

```python
import functools

import jax
import jax.numpy as jnp
from jax import lax
from jax.experimental import pallas as pl
from jax.experimental.pallas import tpu as pltpu

RMS_EPS = 1e-6
NEG_INF = -1e30
MLA_THETA = 10000.0
ROPE_THETA = 500000.0
LANES = 128
SUBLANES = 8
VMEM_LIMIT_BYTES = 56 * 1024 * 1024
MLA_PAGES_PER_STEP = 32
FFN_ROW_TILE = 1024
FFN_ROW_BLOCK = 512
LOG2E = 1.4426950408889634
BF16 = jnp.bfloat16
F32 = jnp.float32


def _params(n_axes):
    return pltpu.CompilerParams(dimension_semantics=("arbitrary",) * n_axes, vmem_limit_bytes=VMEM_LIMIT_BYTES)


def _const_spec(shape):
    nd = len(shape)
    return pl.BlockSpec(shape, lambda *_: (0,) * nd, pipeline_mode=pl.Buffered(1))


def _tile(m, pref):
    t = min(m, pref)
    while m % t or t % SUBLANES:
        t -= 1
    return t


def _dot(a, b):
    return jnp.dot(a, b, preferred_element_type=F32)


def _dot_nt(a, b):
    return lax.dot_general(a, b, (((1,), (1,)), ((), ())), preferred_element_type=F32)


def _rms(x, g):
    return x * lax.rsqrt(jnp.mean(x * x, axis=-1, keepdims=True) + RMS_EPS) * g


def _mla_in_kernel(x_ref, g_ref, w_ref, qg_ref, kvg_ref, tab_ref, rows_ref, cq_ref, *key_refs, q_lora, kv_lora, rope):
    h = _rms(x_ref[...], g_ref[...]).astype(BF16)
    z = _dot(h, w_ref[...])
    cq = _rms(z[:, :q_lora], qg_ref[...])
    ckv = _rms(z[:, q_lora:q_lora + kv_lora], kvg_ref[...])
    t = z[:, q_lora + kv_lora:] * tab_ref[...]
    kpe2 = t + pltpu.roll(t, rope, axis=1)
    cq_ref[...] = cq.astype(BF16)
    rows_ref[:, :kv_lora] = ckv
    rows_ref[:, kv_lora:] = kpe2[:, :rope]
    if key_refs:
        ckv_ref, kt_ref = key_refs
        ckv_ref[...] = ckv.astype(BF16)
        kt_ref[:kv_lora, :] = ckv.T.astype(BF16)
        kt_ref[kv_lora:, :] = kpe2.T[:rope].astype(BF16)


def _mla_in(x, g, w_in_ext, q_g, kv_g, tab, *, q_lora, kv_lora, rope, with_keys):
    m, d = x.shape
    tm = _tile(m, 512)
    n = w_in_ext.shape[1]
    row = lambda i: (i, 0)
    out_specs = [pl.BlockSpec((tm, kv_lora + rope), row), pl.BlockSpec((tm, q_lora), row)]
    out_shape = [jax.ShapeDtypeStruct((m, kv_lora + rope), F32), jax.ShapeDtypeStruct((m, q_lora), BF16)]
    if with_keys:
        out_specs += [pl.BlockSpec((tm, kv_lora), row), pl.BlockSpec((None, kv_lora + rope, tm), lambda i: (i, 0, 0))]
        out_shape += [jax.ShapeDtypeStruct((m, kv_lora), BF16), jax.ShapeDtypeStruct((m // tm, kv_lora + rope, tm), BF16)]
    return pl.pallas_call(
        functools.partial(_mla_in_kernel, q_lora=q_lora, kv_lora=kv_lora, rope=rope),
        grid=(m // tm,),
        in_specs=[
            pl.BlockSpec((tm, d), row),
            _const_spec((1, d)),
            _const_spec((d, n)),
            _const_spec((1, q_lora)),
            _const_spec((1, kv_lora)),
            pl.BlockSpec((tm, 2 * rope), row),
        ],
        out_specs=out_specs,
        out_shape=out_shape,
        compiler_params=_params(1),
        name="mla_in",
    )(x, g, w_in_ext, q_g, kv_g, tab)


def _mla_q_kernel(cq_ref, wuq_ref, wuk_ref, tab_ref, qlat_ref, qpe_ref, *, heads, nope, rope, scale):
    q = _dot(cq_ref[...], wuq_ref[...])
    tab = tab_ref[...]
    for h in range(heads):
        qn = q[:, h * nope:(h + 1) * nope].astype(BF16)
        qlat_ref[h] = (_dot(qn, wuk_ref[h]) * scale).astype(qlat_ref.dtype)
        lo = heads * nope + h * 2 * rope
        t = q[:, lo:lo + 2 * rope] * tab
        qpe_ref[h] = ((t + pltpu.roll(t, rope, axis=1))[:, :rope] * scale).astype(qpe_ref.dtype)


def _mla_q(cq, w_uq_ext, w_uk_t, tab, *, heads, nope, rope, scale, out_dtype):
    m, q_lora = cq.shape
    kv_lora = w_uk_t.shape[2]
    tm = _tile(m, 512)
    return pl.pallas_call(
        functools.partial(_mla_q_kernel, heads=heads, nope=nope, rope=rope, scale=scale),
        grid=(m // tm,),
        in_specs=[
            pl.BlockSpec((tm, q_lora), lambda i: (i, 0)),
            _const_spec(w_uq_ext.shape),
            _const_spec(w_uk_t.shape),
            pl.BlockSpec((tm, 2 * rope), lambda i: (i, 0)),
        ],
        out_specs=[
            pl.BlockSpec((heads, tm, kv_lora), lambda i: (0, i, 0)),
            pl.BlockSpec((heads, tm, rope), lambda i: (0, i, 0)),
        ],
        out_shape=[
            jax.ShapeDtypeStruct((heads, m, kv_lora), out_dtype),
            jax.ShapeDtypeStruct((heads, m, rope), out_dtype),
        ],
        compiler_params=_params(1),
        name="mla_q",
    )(cq, w_uq_ext, w_uk_t, tab)


def _mla_prompt_attn_kernel(qlat_ref, qpe_ref, kt_ref, ckv_ref, o_ref, m_sc, l_sc, acc_sc, *, tq, tk, heads, group):
    i = pl.program_id(1)
    j = pl.program_id(2)
    nk = pl.num_programs(2)
    kv_lora = ckv_ref.shape[1]
    rope = qpe_ref.shape[2]
    last_tile = ((i + 1) * tq - 1) // tk

    @pl.when(j == 0)
    def _():
        m_sc[...] = jnp.full(m_sc.shape, NEG_INF, F32)
        l_sc[...] = jnp.zeros(l_sc.shape, F32)
        acc_sc[...] = jnp.zeros(acc_sc.shape, F32)

    @pl.when(j <= last_tile)
    def _():
        qpos = i * tq + lax.broadcasted_iota(jnp.int32, (tq, tk), 0)
        kpos = j * tk + lax.broadcasted_iota(jnp.int32, (tq, tk), 1)
        bias = jnp.where(kpos <= qpos, 0.0, NEG_INF)
        ckv = ckv_ref[...]
        kt_c = kt_ref[:kv_lora, :]
        kt_p = kt_ref[kv_lora:, :]
        def logits(g):
            hs = slice(g * group, (g + 1) * group)
            q = qlat_ref[hs].reshape(group * tq, kv_lora)
            qp = qpe_ref[hs].reshape(group * tq, rope)
            return _dot(q, kt_c) + _dot(qp, kt_p)

        n_groups = heads // group
        s_next = logits(0)
        for g in range(n_groups):
            s = s_next
            if g + 1 < n_groups:
                s_next = logits(g + 1)
            s = (s.reshape(group, tq, tk) + bias[None]).reshape(group * tq, tk)
            m_prev = m_sc[g]
            m_new = jnp.maximum(m_prev, s.max(axis=1, keepdims=True))
            alpha = jnp.exp2(m_prev - m_new)
            p = jnp.exp2(s - m_new)
            l_sc[g] = alpha * l_sc[g] + p.sum(axis=1, keepdims=True)
            acc_sc[g] = alpha * acc_sc[g] + _dot(p.astype(BF16), ckv)
            m_sc[g] = m_new

    @pl.when(j == nk - 1)
    def _():
        for g in range(heads // group):
            o = acc_sc[g] / l_sc[g]
            o_ref[g * group:(g + 1) * group] = o.reshape(group, tq, kv_lora).astype(o_ref.dtype)


def _mla_prompt_attn(qlat, qpe, kt, ckv, *, batch, seq):
    heads, m, kv_lora = qlat.shape
    rope = qpe.shape[2]
    tq = _tile(seq, 256)
    tk = kt.shape[2]
    assert seq % tk == 0 and tk % tq == 0
    nq, nk = seq // tq, seq // tk
    group = max(1, 512 // tq)
    while heads % group:
        group -= 1

    def key_tile(b, i, j):
        return b * nk + jnp.minimum(j, ((i + 1) * tq - 1) // tk)

    qmap = lambda b, i, j: (0, b * nq + i, 0)
    return pl.pallas_call(
        functools.partial(_mla_prompt_attn_kernel, tq=tq, tk=tk, heads=heads, group=group),
        grid=(batch, nq, nk),
        in_specs=[
            pl.BlockSpec((heads, tq, kv_lora), qmap),
            pl.BlockSpec((heads, tq, rope), qmap),
            pl.BlockSpec((None, kv_lora + rope, tk), lambda b, i, j: (key_tile(b, i, j), 0, 0)),
            pl.BlockSpec((tk, kv_lora), lambda b, i, j: (key_tile(b, i, j), 0)),
        ],
        out_specs=pl.BlockSpec((heads, tq, kv_lora), qmap),
        out_shape=jax.ShapeDtypeStruct((heads, m, kv_lora), BF16),
        scratch_shapes=[
            pltpu.VMEM((heads // group, group * tq, 1), F32),
            pltpu.VMEM((heads // group, group * tq, 1), F32),
            pltpu.VMEM((heads // group, group * tq, kv_lora), F32),
        ],
        compiler_params=_params(3),
        name="mla_prompt_attn",
    )(qlat, qpe, kt, ckv)


def _mla_sample_attn_kernel(pt_ref, qlat_ref, qpe_ref, new_ref, *rest, pg, heads, tq, page):
    del pt_ref
    page_refs = rest[:pg]
    o_ref, kt_sc, m_sc, l_sc, acc_sc = rest[pg:]
    j = pl.program_id(1)
    nj = pl.num_programs(1)
    kv_lora = qlat_ref.shape[2]
    rope = qpe_ref.shape[2]
    rows = heads * tq

    @pl.when(j == 0)
    def _():
        m_sc[...] = jnp.full(m_sc.shape, NEG_INF, F32)
        l_sc[...] = jnp.zeros(l_sc.shape, F32)
        acc_sc[...] = jnp.zeros(acc_sc.shape, F32)

    q = qlat_ref[...].reshape(rows, kv_lora).astype(BF16)
    qp = qpe_ref[...].reshape(rows, rope).astype(BF16)

    def update(s, pv):
        m_prev = m_sc[...]
        m_new = jnp.maximum(m_prev, s.max(axis=1, keepdims=True))
        alpha = jnp.exp2(m_prev - m_new)
        p = jnp.exp2(s - m_new)
        l_sc[...] = alpha * l_sc[...] + p.sum(axis=1, keepdims=True)
        acc_sc[...] = alpha * acc_sc[...] + pv(p.astype(BF16))
        m_sc[...] = m_new

    n_slabs = 2 if pg % 2 == 0 else 1
    per = pg // n_slabs
    logits, values = [], []
    for h in range(n_slabs):
        for k in range(h * per, (h + 1) * per):
            kt_sc[:, k * page:(k + 1) * page] = page_refs[k][...].astype(BF16)
        cols = slice(h * per * page, (h + 1) * per * page)
        ckv_t = kt_sc[:kv_lora, cols]
        logits.append(_dot(q, ckv_t) + _dot(qp, kt_sc[kv_lora:, cols]))
        values.append(ckv_t)
    for s, ckv_t in zip(logits, values):
        update(s, lambda p, v=ckv_t: _dot_nt(p, v))

    @pl.when(j == nj - 1)
    def _():
        k_new = jnp.concatenate([new_ref[...], jnp.zeros((page - tq, kv_lora + rope), F32)], axis=0)
        ckv = k_new[:, :kv_lora].astype(BF16)
        s = _dot_nt(q, ckv) + _dot_nt(qp, k_new[:, kv_lora:].astype(BF16))
        t_q = lax.broadcasted_iota(jnp.int32, (tq, page), 0)
        t_k = lax.broadcasted_iota(jnp.int32, (tq, page), 1)
        s = jnp.where((t_k <= t_q)[None], s.reshape(heads, tq, page), NEG_INF).reshape(rows, page)
        update(s, lambda p: _dot(p, ckv))
        o = acc_sc[...] / l_sc[...]
        o_ref[...] = o.reshape(heads, tq, kv_lora).astype(o_ref.dtype)


def _mla_sample_attn(qlat, qpe, rows_new, cache_t, page_table, *, layer, tq, pages_per_step):
    heads, m, kv_lora = qlat.shape
    rope = qpe.shape[2]
    n_seq, n_pages = page_table.shape
    page = cache_t.shape[3]
    pg = pages_per_step
    while n_pages % pg:
        pg -= 1
    qmap = lambda s, j, pt: (0, s, 0)

    def page_spec(k):
        return pl.BlockSpec((None, None, kv_lora + rope, page), lambda s, j, pt: (layer, pt[s * n_pages + j * pg + k], 0, 0))

    grid_spec = pltpu.PrefetchScalarGridSpec(
        num_scalar_prefetch=1,
        grid=(n_seq, n_pages // pg),
        in_specs=[
            pl.BlockSpec((heads, tq, kv_lora), qmap),
            pl.BlockSpec((heads, tq, rope), qmap),
            pl.BlockSpec((tq, kv_lora + rope), lambda s, j, pt: (s, 0)),
        ] + [page_spec(k) for k in range(pg)],
        out_specs=pl.BlockSpec((heads, tq, kv_lora), qmap),
        scratch_shapes=[
            pltpu.VMEM((kv_lora + rope, pg * page), BF16),
            pltpu.VMEM((heads * tq, 1), F32),
            pltpu.VMEM((heads * tq, 1), F32),
            pltpu.VMEM((heads * tq, kv_lora), F32),
        ],
    )
    return pl.pallas_call(
        functools.partial(_mla_sample_attn_kernel, pg=pg, heads=heads, tq=tq, page=page),
        grid_spec=grid_spec,
        out_shape=jax.ShapeDtypeStruct((heads, m, kv_lora), F32),
        compiler_params=_params(2),
        name="mla_sample_attn",
    )(page_table.reshape(-1), qlat, qpe, rows_new, *([cache_t] * pg))


def _mla_out_kernel(o_ref, wuv_ref, wo_ref, x_ref, g_ref, out_ref, *, heads):
    v = jnp.concatenate([_dot(o_ref[h].astype(BF16), wuv_ref[h]).astype(BF16) for h in range(heads)], axis=1)
    out_ref[...] = x_ref[...] + _rms(_dot(v, wo_ref[...]), g_ref[...])


def _mla_out(o_lat, w_uv_t, w_o, x, g):
    heads, m, kv_lora = o_lat.shape
    d = x.shape[1]
    tm = _tile(m, 512 if o_lat.dtype == BF16 else 256)
    return pl.pallas_call(
        functools.partial(_mla_out_kernel, heads=heads),
        grid=(m // tm,),
        in_specs=[
            pl.BlockSpec((heads, tm, kv_lora), lambda i: (0, i, 0)),
            _const_spec(w_uv_t.shape),
            _const_spec(w_o.shape),
            pl.BlockSpec((tm, d), lambda i: (i, 0)),
            _const_spec((1, d)),
        ],
        out_specs=pl.BlockSpec((tm, d), lambda i: (i, 0)),
        out_shape=jax.ShapeDtypeStruct((m, d), F32),
        compiler_params=_params(1),
        name="mla_out",
    )(o_lat, w_uv_t, w_o, x, g)


def _out_proj_kernel(a_ref, w_ref, b_ref, x_ref, g_ref, out_ref):
    y = _dot(a_ref[...].astype(BF16), w_ref[...]) + b_ref[...]
    out_ref[...] = x_ref[...] + _rms(y, g_ref[...])


def _out_proj(a, w, b, x, g):
    m, k = a.shape
    d = x.shape[1]
    tm = _tile(m, 512)
    return pl.pallas_call(
        _out_proj_kernel,
        grid=(m // tm,),
        in_specs=[
            pl.BlockSpec((tm, k), lambda i: (i, 0)),
            _const_spec(w.shape),
            _const_spec((1, d)),
            pl.BlockSpec((tm, d), lambda i: (i, 0)),
            _const_spec((1, d)),
        ],
        out_specs=pl.BlockSpec((tm, d), lambda i: (i, 0)),
        out_shape=jax.ShapeDtypeStruct((m, d), F32),
        compiler_params=_params(1),
        name="out_proj",
    )(a, w, b, x, g)


def _norm_matmul_kernel(*refs, has_bias, has_rope, rot_half, scale):
    x_ref, g_ref, w_ref = refs[:3]
    refs = refs[3:]
    if has_bias:
        b_ref, refs = refs[0], refs[1:]
    if has_rope:
        (c_ref, s1_ref, s2_ref), refs = refs[:3], refs[3:]
    out_ref, h_sc = refs

    @pl.when(pl.program_id(1) == 0)
    def _():
        h_sc[...] = _rms(x_ref[...], g_ref[...]).astype(BF16)

    y = _dot(h_sc[...], w_ref[...])
    if has_bias:
        y = y + b_ref[...]
    if has_rope:
        c, s1, s2 = c_ref[...], s1_ref[...], s2_ref[...]
        for k in range(y.shape[1] // LANES):
            yc = y[:, k * LANES:(k + 1) * LANES]
            yc = yc * c + pltpu.roll(yc, LANES - rot_half, axis=1) * s1 + pltpu.roll(yc, rot_half, axis=1) * s2
            out_ref[:, k * LANES:(k + 1) * LANES] = (yc * scale).astype(out_ref.dtype)
    else:
        out_ref[...] = (y * scale).astype(out_ref.dtype) if scale != 1.0 else y.astype(out_ref.dtype)


def _norm_matmul(x, g, w, b=None, rope_tabs=None, *, rot_half=0, scale=1.0, out_dtype=F32, tm_pref=1024, tn_pref=1024):
    m, d = x.shape
    n = w.shape[1]
    tm = _tile(m, tm_pref)
    tn = min(n, tn_pref)
    while n % tn or tn % LANES:
        tn -= LANES
    in_specs = [
        pl.BlockSpec((tm, d), lambda i, j: (i, 0)),
        _const_spec((1, d)),
        pl.BlockSpec((d, tn), lambda i, j: (0, j)),
    ]
    args = [x, g, w]
    if b is not None:
        in_specs.append(pl.BlockSpec((1, tn), lambda i, j: (0, j)))
        args.append(b)
    if rope_tabs is not None:
        in_specs += [pl.BlockSpec((tm, LANES), lambda i, j: (i, 0))] * 3
        args += list(rope_tabs)
    return pl.pallas_call(
        functools.partial(_norm_matmul_kernel, has_bias=b is not None, has_rope=rope_tabs is not None, rot_half=rot_half, scale=scale),
        grid=(m // tm, n // tn),
        in_specs=in_specs,
        out_specs=pl.BlockSpec((tm, tn), lambda i, j: (i, j)),
        out_shape=jax.ShapeDtypeStruct((m, n), out_dtype),
        scratch_shapes=[pltpu.VMEM((tm, d), BF16)],
        compiler_params=_params(2),
        name="norm_matmul",
    )(*args)


def _ffn_kernel(*refs, per_seq_rows, tiles_per_seq):
    if tiles_per_seq:
        x_ref, gpre_ref, wg_ref, wu_ref, cw_ref, cb_ref, wd_ref, gpost_ref, out_ref, tail_ref, h_sc, acc_sc, carry_sc = refs
    else:
        x_ref, gpre_ref, wg_ref, wu_ref, prev_ref, cw_ref, cb_ref, wd_ref, gpost_ref, out_ref, tail_ref, h_sc, acc_sc = refs
    i = pl.program_id(0)
    f = pl.program_id(1)
    nf = pl.num_programs(1)

    @pl.when(f == 0)
    def _():
        h_sc[...] = _rms(x_ref[...], gpre_ref[...]).astype(BF16)
        acc_sc[...] = jnp.zeros(acc_sc.shape, F32)

    tm = h_sc.shape[0]
    unit = SUBLANES if tiles_per_seq else per_seq_rows
    half = FFN_ROW_BLOCK if tm % FFN_ROW_BLOCK == 0 and FFN_ROW_BLOCK % unit == 0 else tm
    blocks = [slice(k * half, (k + 1) * half) for k in range(tm // half)]
    gates, ups = [], []
    for rows in blocks:
        hb = h_sc[rows, :]
        gates.append(_dot(hb, wg_ref[...]))
        ups.append(_dot(hb, wu_ref[...]))
    last8 = gates[-1][half - SUBLANES:, :]
    if tiles_per_seq:
        tail_ref[...] = last8

        @pl.when(i % tiles_per_seq == 0)
        def _():
            carry_sc[f] = jnp.zeros(last8.shape, F32)

        before = carry_sc[f]
        carry_sc[f] = last8
    for k, rows in enumerate(blocks):
        gate, up = gates[k], ups[k]
        r = lax.broadcasted_iota(jnp.int32, gate.shape, 0)
        d1 = pltpu.roll(gate, 1, axis=0)
        d2 = pltpu.roll(gate, 2, axis=0)
        if tiles_per_seq:
            h6 = before[SUBLANES - 2:SUBLANES - 1, :]
            h7 = before[SUBLANES - 1:SUBLANES, :]
            g1 = jnp.where(r == 0, h7, d1)
            g2 = jnp.where(r == 0, h6, jnp.where(r == 1, h7, d2))
            before = gate[half - SUBLANES:, :]
        else:
            tail_ref[rows, :] = gate
            t = r % per_seq_rows
            p2 = prev_ref[rows, :]
            g1 = jnp.where(t == 0, pltpu.roll(p2, half - 1, axis=0), d1)
            g2 = jnp.where(t < 2, p2, d2)
        conv = cb_ref[...] + g2 * cw_ref[0:1, :] + g1 * cw_ref[1:2, :] + gate * cw_ref[2:3, :]
        act = conv * (1.0 / (1.0 + jnp.exp(-conv))) * up
        acc_sc[rows, :] += _dot(act.astype(BF16), wd_ref[...])

    @pl.when(f == nf - 1)
    def _():
        out_ref[...] = x_ref[...] + _rms(acc_sc[...], gpost_ref[...])


def _ffn(x, g_pre, w_up, prev, conv_w, conv_b, w_down, g_post, *, layer, seq_rows):
    m, d = x.shape
    d_ff = w_down.shape[1]
    tf = _tile(d_ff, 512)
    nf = d_ff // tf
    tm = _tile(seq_rows, FFN_ROW_TILE) if prev is None else _tile(m, FFN_ROW_TILE // 2)
    in_specs = [
        pl.BlockSpec((tm, d), lambda i, f: (i, 0), pipeline_mode=pl.Buffered(1)),
        _const_spec((1, d)),
        pl.BlockSpec((None, d, tf), lambda i, f: (layer, 0, f)),
        pl.BlockSpec((None, d, tf), lambda i, f: (layer, 0, nf + f)),
    ]
    args = [x, g_pre, w_up, w_up]
    scratch = [pltpu.VMEM((tm, d), BF16), pltpu.VMEM((tm, d), F32)]
    if prev is None:
        tiles_per_seq = seq_rows // tm
        scratch.append(pltpu.VMEM((nf, SUBLANES, tf), F32))
        tail_spec = pl.BlockSpec((None, SUBLANES, tf), lambda i, f: (i, 0, f))
        tail_shape = jax.ShapeDtypeStruct((m // tm, SUBLANES, d_ff), F32)
    else:
        assert tm % seq_rows == 0 and seq_rows >= 2
        tiles_per_seq = 0
        in_specs.append(pl.BlockSpec((tm, tf), lambda i, f: (i, f)))
        args.append(prev)
        tail_spec = pl.BlockSpec((tm, tf), lambda i, f: (i, f))
        tail_shape = jax.ShapeDtypeStruct((m, d_ff), F32)
    in_specs += [
        pl.BlockSpec((None, conv_w.shape[1], tf), lambda i, f: (layer, 0, f)),
        pl.BlockSpec((None, 1, tf), lambda i, f: (layer, 0, f)),
        pl.BlockSpec((None, tf, d), lambda i, f: (layer, f, 0)),
        _const_spec((1, d)),
    ]
    args += [conv_w, conv_b, w_down, g_post]
    return pl.pallas_call(
        functools.partial(_ffn_kernel, per_seq_rows=seq_rows, tiles_per_seq=tiles_per_seq),
        grid=(m // tm, nf),
        in_specs=in_specs,
        out_specs=[pl.BlockSpec((tm, d), lambda i, f: (i, 0), pipeline_mode=pl.Buffered(1)), tail_spec],
        out_shape=[jax.ShapeDtypeStruct((m, d), F32), tail_shape],
        scratch_shapes=scratch,
        compiler_params=_params(2),
        name="ffn",
    )(*args)


def _sink_softmax(s_list, sink, out_dtype=BF16):
    m = sink
    for s in s_list:
        m = jnp.maximum(m, s.max(axis=1, keepdims=True))
    denom = jnp.exp(sink - m)
    es = []
    for s in s_list:
        e = jnp.exp(s - m)
        es.append(e)
        denom = denom + e.sum(axis=1, keepdims=True)
    return [(e / denom).astype(out_dtype) for e in es]


def _swa_prompt_kernel(sink_ref, q_ref, kc_ref, kp_ref, vc_ref, vp_ref, o_ref, *, kv_heads, group, hd):
    i = pl.program_id(1)
    w = q_ref.shape[0]
    r = lax.broadcasted_iota(jnp.int32, (w, w), 0)
    c = lax.broadcasted_iota(jnp.int32, (w, w), 1)
    in_cur = c <= r
    has_prev = i > 0

    def scores(kv):
        sl = slice(kv * hd, (kv + 1) * hd)
        k_cur = kc_ref[:, sl].astype(BF16)
        k_prev = kp_ref[:, sl].astype(BF16)
        out = []
        for g in range(group):
            qh = q_ref[:, (kv * group + g) * hd:(kv * group + g + 1) * hd]
            out.append(jnp.where(in_cur, _dot_nt(qh, k_cur), jnp.where(has_prev, _dot_nt(qh, k_prev), NEG_INF)))
        return out

    s_next = scores(0)
    for kv in range(kv_heads):
        s_cur = s_next
        if kv + 1 < kv_heads:
            s_next = scores(kv + 1)
        sl = slice(kv * hd, (kv + 1) * hd)
        v_cur = vc_ref[:, sl].astype(BF16)
        v_prev = vp_ref[:, sl].astype(BF16)
        outs = []
        for g in range(group):
            p = _sink_softmax([s_cur[g]], sink_ref[kv * group + g], out_dtype=F32)[0]
            outs.append(_dot(jnp.where(in_cur, p, 0.0).astype(BF16), v_cur) + _dot(jnp.where(in_cur, 0.0, p).astype(BF16), v_prev))
        o_ref[:, kv * group * hd:(kv + 1) * group * hd] = jnp.concatenate(outs, axis=1).astype(o_ref.dtype)


def _swa_prompt_attn(q, k, v, sinks, *, batch, seq, window, kv_heads, hd):
    m, dq = q.shape
    dk = k.shape[1]
    nb = seq // window
    group = dq // (kv_heads * hd)
    cur = lambda b, i, sk: (b * nb + i, 0)
    prev = lambda b, i, sk: (jnp.maximum(b * nb + i - 1, 0), 0)
    grid_spec = pltpu.PrefetchScalarGridSpec(
        num_scalar_prefetch=1,
        grid=(batch, nb),
        in_specs=[
            pl.BlockSpec((window, dq), cur),
            pl.BlockSpec((window, dk), cur),
            pl.BlockSpec((window, dk), prev),
            pl.BlockSpec((window, dk), cur),
            pl.BlockSpec((window, dk), prev),
        ],
        out_specs=pl.BlockSpec((window, dq), cur),
    )
    return pl.pallas_call(
        functools.partial(_swa_prompt_kernel, kv_heads=kv_heads, group=group, hd=hd),
        grid_spec=grid_spec,
        out_shape=jax.ShapeDtypeStruct((m, dq), BF16),
        compiler_params=_params(2),
        name="swa_prompt_attn",
    )(sinks.reshape(-1), q, k, k, v, v)


def _swa_sample_kernel(sink_ref, q_ref, st_ref, kn_ref, vn_ref, o_ref, *, tq, kv_heads, group, hd):
    n_seq, w, _ = st_ref.shape
    dk = kv_heads * hd
    rows = group * tq
    t_q = lax.broadcasted_iota(jnp.int32, (rows, w), 0) % tq
    col = lax.broadcasted_iota(jnp.int32, (rows, w), 1)
    grp = lax.broadcasted_iota(jnp.int32, (rows, 1), 0) // tq
    pad = jnp.zeros((w - tq, hd), F32)
    sinks = []
    for kv in range(kv_heads):
        sink = jnp.zeros((rows, 1), F32)
        for g in range(group):
            sink = jnp.where(grp == g, sink_ref[kv * group + g], sink)
        sinks.append(sink)
    scores = []
    for b in range(n_seq):
        rs = slice(b * tq, (b + 1) * tq)
        for kv in range(kv_heads):
            sl = slice(kv * hd, (kv + 1) * hd)
            k_old = st_ref[b, :, sl].astype(BF16)
            k_new = jnp.concatenate([kn_ref[rs, sl], pad], axis=0).astype(BF16)
            qs = jnp.concatenate([q_ref[rs, (kv * group + g) * hd:(kv * group + g + 1) * hd] for g in range(group)], axis=0).astype(BF16)
            scores.append([jnp.where(col > t_q, _dot_nt(qs, k_old), NEG_INF), jnp.where(col <= t_q, _dot_nt(qs, k_new), NEG_INF)])
    probs = [_sink_softmax(s, sinks[n % kv_heads]) for n, s in enumerate(scores)]
    for b in range(n_seq):
        rs = slice(b * tq, (b + 1) * tq)
        outs = []
        for kv in range(kv_heads):
            sl = slice(kv * hd, (kv + 1) * hd)
            v_old = st_ref[b, :, dk + kv * hd:dk + (kv + 1) * hd].astype(BF16)
            v_new = jnp.concatenate([vn_ref[rs, sl], pad], axis=0).astype(BF16)
            p_old, p_new = probs[b * kv_heads + kv]
            o = _dot(p_old, v_old) + _dot(p_new, v_new)
            outs += [o[g * tq:(g + 1) * tq] for g in range(group)]
        o_ref[rs, :] = jnp.concatenate(outs, axis=1).astype(o_ref.dtype)


def _swa_sample_attn(q, state, k_new, v_new, sinks, *, tq, kv_heads, hd):
    m, dq = q.shape
    n_seq, window, dkv = state.shape
    dk = k_new.shape[1]
    group = dq // (kv_heads * hd)
    nb = 4
    while n_seq % nb:
        nb -= 1
    row = lambda s, sk: (s, 0)
    grid_spec = pltpu.PrefetchScalarGridSpec(
        num_scalar_prefetch=1,
        grid=(n_seq // nb,),
        in_specs=[
            pl.BlockSpec((nb * tq, dq), row),
            pl.BlockSpec((nb, window, dkv), lambda s, sk: (s, 0, 0)),
            pl.BlockSpec((nb * tq, dk), row),
            pl.BlockSpec((nb * tq, dk), row),
        ],
        out_specs=pl.BlockSpec((nb * tq, dq), row),
    )
    return pl.pallas_call(
        functools.partial(_swa_sample_kernel, tq=tq, kv_heads=kv_heads, group=group, hd=hd),
        grid_spec=grid_spec,
        out_shape=jax.ShapeDtypeStruct((m, dq), F32),
        compiler_params=_params(1),
        name="swa_sample_attn",
    )(sinks.reshape(-1), q, state, k_new, v_new)


def _rope_tables(pos, dim, theta):
    inv = theta ** (-jnp.arange(0, dim, 2, dtype=F32) / dim)
    ang = pos.astype(F32)[:, None] * inv[None, :]
    return jnp.cos(ang), jnp.sin(ang)


def _rot_cols(w, half):
    return jnp.concatenate([-w[..., half:], w[..., :half]], axis=-1)


def kernel(x_prompt, x_sample, cache_mla, state_swa_kv, state_ffn, page_table, ln_mix_pre, ln_mix_post, ln_ffn_pre, ln_ffn_post, w_ffn_up, ffn_conv_w, ffn_conv_b, w_ffn_down, w_mla_in, mla_q_norm, mla_kv_norm, w_mla_uq, w_mla_uk, w_mla_uv, w_mla_o, swa_kv_norm, w_swa_kv, b_swa_kv, w_swa_q, b_swa_q, swa_sinks, w_swa_o, b_swa_o):
    batch, seq, d = x_prompt.shape
    n_seq, tq_s, _ = x_sample.shape
    depth = ln_mix_pre.shape[0]
    n_a = w_mla_in.shape[0]
    kv_lora, heads, nope = w_mla_uk.shape[1:]
    v_dim = w_mla_uv.shape[3]
    q_lora = mla_q_norm.shape[1]
    rope = w_mla_in.shape[2] - q_lora - kv_lora
    page = cache_mla.shape[2]
    past_len = page_table.shape[1] * page
    window, _, kv_heads, hd = state_swa_kv.shape[1:]
    dk = kv_heads * hd
    rot_half = hd // 8
    d_ff = w_ffn_down.shape[1]
    mla_scale = float((nope + rope) ** -0.5)
    swa_scale = float(hd ** -0.5)
    assert 2 * rope == LANES and 2 * hd == LANES and tq_s == SUBLANES and seq % window == 0

    xs = {"p": x_prompt.reshape(batch * seq, d), "s": x_sample.reshape(n_seq * tq_s, d)}
    pos = {"p": jnp.tile(jnp.arange(seq), batch), "s": jnp.tile(past_len + jnp.arange(tq_s), n_seq)}
    mla_tab, swa_tabs = {}, {}
    for st in ("p", "s"):
        cos, sin = _rope_tables(pos[st], rope, MLA_THETA)
        mla_tab[st] = jnp.concatenate([cos, cos, sin, sin], axis=1)
        cos, sin = _rope_tables(pos[st], 2 * rot_half, ROPE_THETA)
        one = jnp.ones((cos.shape[0], hd - 2 * rot_half), F32)
        zero = jnp.zeros_like(one)
        z8 = jnp.zeros_like(sin)
        swa_tabs[st] = tuple(
            jnp.tile(jnp.concatenate(parts, axis=1), (1, LANES // hd))
            for parts in ([cos, cos, one], [-sin, z8, zero], [z8, sin, zero])
        )
    row2 = lambda a: a.reshape(1, -1)

    mla_rows = {"p": [], "s": []}
    ffn_state = {"p": [], "s": []}
    k_sh, v_sh = {}, {}
    swa_state = state_swa_kv.reshape(n_seq, window, 2 * dk)
    cache_t = jnp.swapaxes(cache_mla, 2, 3)
    w_up_bf = w_ffn_up.astype(BF16)
    w_down_bf = w_ffn_down.astype(BF16)
    conv_b3 = ffn_conv_b.reshape(depth, 1, d_ff)
    for l in range(depth):
        if l < n_a:
            a = l
            w_in = w_mla_in[a]
            w_pe = w_in[:, q_lora + kv_lora:]
            w_in_ext = jnp.concatenate([w_in, _rot_cols(w_pe, rope // 2)], axis=1).astype(BF16)
            w_uq = w_mla_uq[a].reshape(q_lora, heads, nope + rope)
            w_uq_pe = w_uq[..., nope:]
            w_uq_ext = jnp.concatenate(
                [w_uq[..., :nope].reshape(q_lora, heads * nope),
                 jnp.concatenate([w_uq_pe, _rot_cols(w_uq_pe, rope // 2)], axis=-1).reshape(q_lora, heads * 2 * rope)],
                axis=1).astype(BF16)
            w_uk_t = jnp.transpose(w_mla_uk[a], (1, 2, 0)).astype(BF16)
            w_uv_t = jnp.transpose(w_mla_uv[a], (1, 0, 2)).astype(BF16)
            w_o = w_mla_o[a].astype(BF16)
            for st in ("p", "s"):
                rows, cq, *keys = _mla_in(xs[st], row2(ln_mix_pre[l]), w_in_ext, row2(mla_q_norm[a]), row2(mla_kv_norm[a]),
                                          mla_tab[st], q_lora=q_lora, kv_lora=kv_lora, rope=rope, with_keys=st == "p")
                mla_rows[st].append(rows)
                qlat, qpe = _mla_q(cq, w_uq_ext, w_uk_t, mla_tab[st], heads=heads, nope=nope, rope=rope, scale=mla_scale * LOG2E,
                                   out_dtype=BF16 if st == "p" else F32)
                if st == "p":
                    ckv, kt = keys
                    o_lat = _mla_prompt_attn(qlat, qpe, kt, ckv, batch=batch, seq=seq)
                else:
                    o_lat = _mla_sample_attn(qlat, qpe, rows, cache_t, page_table, layer=a, tq=tq_s, pages_per_step=MLA_PAGES_PER_STEP)
                xs[st] = _mla_out(o_lat, w_uv_t, w_o, xs[st], row2(ln_mix_post[l]))
        else:
            b = l - n_a
            if l == n_a:
                w_k = w_swa_kv[:, :dk].astype(BF16)
                w_v = w_swa_kv[:, dk:].astype(BF16)
                for st in ("p", "s"):
                    k_sh[st] = _norm_matmul(xs[st], row2(swa_kv_norm), w_k, row2(b_swa_kv[:dk]), swa_tabs[st], rot_half=rot_half)
                    v_sh[st] = _norm_matmul(xs[st], row2(swa_kv_norm), w_v, row2(b_swa_kv[dk:]))
            w_q = w_swa_q[b].astype(BF16)
            w_o = w_swa_o[b].astype(BF16)
            for st in ("p", "s"):
                q = _norm_matmul(xs[st], row2(ln_mix_pre[l]), w_q, row2(b_swa_q[b]), swa_tabs[st], rot_half=rot_half,
                                 scale=swa_scale, out_dtype=BF16 if st == "p" else F32)
                if st == "p":
                    attn = _swa_prompt_attn(q, k_sh[st], v_sh[st], swa_sinks[b], batch=batch, seq=seq, window=window,
                                            kv_heads=kv_heads, hd=hd)
                else:
                    attn = _swa_sample_attn(q, swa_state, k_sh[st], v_sh[st], swa_sinks[b], tq=tq_s, kv_heads=kv_heads, hd=hd)
                xs[st] = _out_proj(attn, w_o, row2(b_swa_o[b]), xs[st], row2(ln_mix_post[l]))
        for st in ("p", "s"):
            if st == "p":
                prev = None
            else:
                prev = jnp.pad(state_ffn[l], ((0, 0), (0, tq_s - state_ffn.shape[2]), (0, 0))).reshape(n_seq * tq_s, d_ff)
            xs[st], tail = _ffn(xs[st], row2(ln_ffn_pre[l]), w_up_bf, prev, ffn_conv_w, conv_b3, w_down_bf,
                                row2(ln_ffn_post[l]), layer=l, seq_rows=seq if st == "p" else tq_s)
            if st == "p":
                ffn_state[st].append(tail.reshape(batch, -1, SUBLANES, d_ff)[:, -1, SUBLANES - 2:])
            else:
                ffn_state[st].append(tail.reshape(n_seq, tq_s, d_ff)[:, tq_s - 2:])

    kv_p = jnp.stack([a.reshape(batch, seq, kv_heads, hd)[:, seq - window:] for a in (k_sh["p"], v_sh["p"])], axis=2)
    kv_s_new = jnp.stack([k_sh["s"].reshape(n_seq, tq_s, kv_heads, hd), v_sh["s"].reshape(n_seq, tq_s, kv_heads, hd)], axis=2)
    return (
        xs["p"].reshape(batch, seq, d),
        xs["s"].reshape(n_seq, tq_s, d),
        jnp.stack(mla_rows["p"], axis=0).reshape(n_a, batch, seq, kv_lora + rope),
        jnp.stack(mla_rows["s"], axis=0).reshape(n_a, n_seq, tq_s, kv_lora + rope),
        kv_p,
        jnp.concatenate([state_swa_kv, kv_s_new], axis=1)[:, tq_s:],
        jnp.stack(ffn_state["p"], axis=0),
        jnp.stack(ffn_state["s"], axis=0),
    )
```
